```python
import jax, jax.numpy as jnp
from jax import lax
import numpy as np

D_MODEL = 1024
BATCH = 8
SEQ = 4096
DEPTH = 1

GLA_HEADS = 4
GLA_DK = D_MODEL // 2 // GLA_HEADS
GLA_DV = D_MODEL // GLA_HEADS
GLA_QK = GLA_HEADS * GLA_DK
GLA_VW = GLA_HEADS * GLA_DV
GLA_LOWRANK = 16
GLA_TAU = 16.0
GLA_CHUNK = 64
SB_HEADS = 8
SB_DH = D_MODEL // SB_HEADS
SB_W = SB_HEADS * SB_DH
SB_BLOCK = 128
D_FF = -(-(8 * D_MODEL) // (3 * 256)) * 256
ALPHA = (2.0 * DEPTH) ** 0.25
BETA = (8.0 * DEPTH) ** -0.25
LN_EPS = 1e-5
IN_SPLITS = (GLA_QK, GLA_QK, GLA_VW, GLA_VW, GLA_LOWRANK, SB_W, SB_W, SB_W, D_MODEL, D_MODEL)
IN_WIDTH = 8208

kernel_name = "hybrid_gla_stickbreaking_deepnorm_adaln"


def _layer_norm(x):
    xf = x.astype(jnp.float32)
    mu = jnp.mean(xf, axis=-1, keepdims=True)
    var = jnp.mean(jnp.square(xf - mu), axis=-1, keepdims=True)
    return (xf - mu) * lax.rsqrt(var + LN_EPS)


def _rms_norm(x):
    xf = x.astype(jnp.float32)
    return xf * lax.rsqrt(jnp.mean(jnp.square(xf), axis=-1, keepdims=True) + LN_EPS)


def _split_cols(h):
    idx = [int(i) for i in np.cumsum(IN_SPLITS)[:-1]]
    return jnp.split(h, idx, axis=-1)


def _gla(q, k, v, log_a):
    B, S, H, dk = q.shape
    dv = v.shape[-1]
    n = S // GLA_CHUNK

    def to_chunks(t):
        return t.reshape(B, n, GLA_CHUNK, H, t.shape[-1]).transpose(1, 0, 3, 2, 4)

    qc = to_chunks(q.astype(jnp.float32))
    kc = to_chunks(k.astype(jnp.float32))
    vc = to_chunks(v.astype(jnp.float32))
    bc = jnp.cumsum(to_chunks(log_a.astype(jnp.float32)), axis=3)
    causal = jnp.tril(jnp.ones((GLA_CHUNK, GLA_CHUNK), dtype=bool))

    def step(state, inp):
        qi, ki, vi, bi = inp
        o_inter = jnp.einsum('bhcd,bhde->bhce', qi * jnp.exp(bi), state)
        diff = bi[:, :, :, None, :] - bi[:, :, None, :, :]
        decay = jnp.exp(jnp.where(causal[:, :, None], diff, -jnp.inf))
        scores = jnp.einsum('bhtd,bhsd,bhtsd->bhts', qi, ki, decay)
        o_intra = jnp.einsum('bhts,bhse->bhte', scores, vi)
        b_last = bi[:, :, -1:, :]
        k_dec = ki * jnp.exp(b_last - bi)
        state = state * jnp.exp(b_last)[:, :, 0, :, None] + jnp.einsum('bhcd,bhce->bhde', k_dec, vi)
        return state, o_inter + o_intra

    state0 = jnp.zeros((B, H, dk, dv), jnp.float32)
    _, o = lax.scan(step, state0, (qc, kc, vc, bc))
    return o.transpose(1, 0, 3, 2, 4).reshape(B, S, H, dv)


def _stick_breaking(q, k, v):
    S = q.shape[2]
    qf, kf, vf = q.astype(jnp.float32), k.astype(jnp.float32), v.astype(jnp.float32)
    outs = []
    for blk in range(S // SB_BLOCK):
        q0 = blk * SB_BLOCK
        kend = q0 + SB_BLOCK
        z = jnp.einsum('bhtd,bhsd->bhts', qf[:, :, q0:kend], kf[:, :, :kend]) * (SB_DH ** -0.5)
        t_idx = q0 + jnp.arange(SB_BLOCK)[:, None]
        s_idx = jnp.arange(kend)[None, :]
        mask = s_idx < t_idx
        log_keep = jnp.where(mask, jax.nn.log_sigmoid(-z), 0.0)
        after = lax.cumsum(log_keep, axis=3, reverse=True) - log_keep
        w = jnp.where(mask, jnp.exp(jax.nn.log_sigmoid(z) + after), 0.0)
        outs.append(jnp.einsum('bhts,bhse->bhte', w, vf[:, :, :kend]))
    return jnp.concatenate(outs, axis=2)


def setup_inputs(seed: int = 0) -> dict:
    key = jax.random.key(seed)
    ks = jax.random.split(key, 16)

    def nrm(k, shape, s):
        return jax.random.normal(k, shape, jnp.float32) * s

    L = DEPTH
    return {
        "x": nrm(ks[0], (BATCH, SEQ, D_MODEL), 1.0),
        "c": nrm(ks[1], (BATCH, D_MODEL), 1.0),
        "w_ada": nrm(ks[2], (L, D_MODEL, 6 * D_MODEL), 0.5 * D_MODEL ** -0.5),
        "b_ada": nrm(ks[3], (L, 6 * D_MODEL), 0.02),
        "w_in": nrm(ks[4], (L, D_MODEL, IN_WIDTH), D_MODEL ** -0.5),
        "w_alpha_up": nrm(ks[5], (L, GLA_LOWRANK, GLA_QK), GLA_LOWRANK ** -0.5),
        "b_alpha": nrm(ks[6], (L, GLA_QK), 0.1),
        "gla_norm_g": 1.0 + nrm(ks[7], (L, GLA_VW), 0.02),
        "w_out": nrm(ks[8], (L, D_MODEL, D_MODEL), BETA * D_MODEL ** -0.5),
        "ln1_g": 1.0 + nrm(ks[9], (L, D_MODEL), 0.02),
        "ln1_b": nrm(ks[10], (L, D_MODEL), 0.02),
        "w_ffn_in": nrm(ks[11], (L, D_MODEL, 2 * D_FF), D_MODEL ** -0.5),
        "w_ffn_out": nrm(ks[12], (L, D_FF, D_MODEL), BETA * D_FF ** -0.5),
        "ln2_g": 1.0 + nrm(ks[13], (L, D_MODEL), 0.02),
        "ln2_b": nrm(ks[14], (L, D_MODEL), 0.02),
    }


def reference(x, c, w_ada, b_ada, w_in, w_alpha_up, b_alpha, gla_norm_g, w_out,
              ln1_g, ln1_b, w_ffn_in, w_ffn_out, ln2_g, ln2_b):
    dtype = x.dtype
    B, S, D = x.shape
    c_act = jax.nn.silu(c)
    for l in range(DEPTH):
        mod = c_act @ w_ada[l] + b_ada[l]
        shift1, scale1, gate1, shift2, scale2, gate2 = [m[:, None, :] for m in jnp.split(mod, 6, axis=-1)]

        u = (_layer_norm(x) * (1.0 + scale1) + shift1).astype(dtype)
        h = u @ w_in[l]
        gq, gk, gv, gg, ga, sq, sk, sv, ma, mb = _split_cols(h)

        log_a = jax.nn.log_sigmoid((ga @ w_alpha_up[l] + b_alpha[l]).astype(jnp.float32)) / GLA_TAU
        q_g = gq.reshape(B, S, GLA_HEADS, GLA_DK) * (GLA_DK ** -0.5)
        k_g = gk.reshape(B, S, GLA_HEADS, GLA_DK)
        v_g = gv.reshape(B, S, GLA_HEADS, GLA_DV)
        o_g = _gla(q_g, k_g, v_g, log_a.reshape(B, S, GLA_HEADS, GLA_DK))
        o_g = _rms_norm(o_g).reshape(B, S, GLA_VW) * gla_norm_g[l] * jax.nn.silu(gg.astype(jnp.float32))

        def to_bhsd(t):
            return t.reshape(B, S, SB_HEADS, SB_DH).transpose(0, 2, 1, 3)
        o_s = _stick_breaking(to_bhsd(sq), to_bhsd(sk), to_bhsd(sv))
        o_s = o_s.transpose(0, 2, 1, 3).reshape(B, S, SB_W)

        merged = (jax.nn.sigmoid(ma.astype(jnp.float32)) * o_g
                  + jax.nn.sigmoid(mb.astype(jnp.float32)) * o_s).astype(dtype)
        y = merged @ w_out[l]
        x = (_layer_norm(ALPHA * x + (1.0 + gate1) * y) * ln1_g[l] + ln1_b[l]).astype(dtype)

        u2 = (_layer_norm(x) * (1.0 + scale2) + shift2).astype(dtype)
        g_ff, up_ff = jnp.split(u2 @ w_ffn_in[l], 2, axis=-1)
        f = (jax.nn.silu(g_ff) * up_ff) @ w_ffn_out[l]
        x = (_layer_norm(ALPHA * x + (1.0 + gate2) * f) * ln2_g[l] + ln2_b[l]).astype(dtype)
    return x
```

```python
import functools

import jax
import jax.numpy as jnp
from jax import lax
from jax.experimental import pallas as pl
from jax.experimental.pallas import tpu as pltpu

F32 = jnp.float32
BF16 = jnp.bfloat16

D_MODEL = 1024
GLA_HEADS = 4
GLA_DK = 128
GLA_DV = 256
GLA_QK = GLA_HEADS * GLA_DK
GLA_VW = GLA_HEADS * GLA_DV
GLA_LOWRANK = 16
GLA_TAU = 16.0
GLA_CHUNK = 64
SB_HEADS = 8
SB_DH = 128
D_FF = 2816
DEPTH = 1
ALPHA = (2.0 * DEPTH) ** 0.25
LN_EPS = 1e-5

HF_W = 3 * GLA_QK
HB_W = 7 * D_MODEL
IN_TN = 512
HF_TILES = HF_W // IN_TN
HB_TILES = HB_W // IN_TN
HB_GV, HB_GG, HB_SQ, HB_SK, HB_SV, HB_MA, HB_MB = 0, 1, 2, 3, 4, 5, 6

VMEM_LIMIT = 56 * 1024 * 1024
LN_ROWS = 256


def _ln(x):
    mu = jnp.mean(x, axis=-1, keepdims=True)
    xc = x - mu
    var = jnp.mean(xc * xc, axis=-1, keepdims=True)
    return xc * lax.rsqrt(var + LN_EPS)


def _sigmoid(x):
    return 1.0 / (1.0 + jnp.exp(-x))


def _log_sigmoid(x):
    return jnp.minimum(x, 0.0) - jnp.log(1.0 + jnp.exp(-jnp.abs(x)))


def _adaln_kernel(c_ref, w_ref, b_ref, o_ref):
    c = c_ref[...]
    ca = c * _sigmoid(c)
    o_ref[...] = jnp.dot(ca, w_ref[...], preferred_element_type=F32,
                         precision=lax.Precision.HIGHEST) + b_ref[...]


def _adaln_mod(c, w_ada, b_ada):
    B, D = c.shape
    N = w_ada.shape[1]
    tn = 1024
    return pl.pallas_call(
        _adaln_kernel,
        grid=(N // tn,),
        in_specs=[pl.BlockSpec((B, D), lambda j: (0, 0)),
                  pl.BlockSpec((D, tn), lambda j: (0, j)),
                  pl.BlockSpec((1, tn), lambda j: (0, j))],
        out_specs=pl.BlockSpec((B, tn), lambda j: (0, j)),
        out_shape=jax.ShapeDtypeStruct((B, N), F32),
        name="adaln_mod",
    )(c, w_ada, b_ada.reshape(1, N))


def _decay_weight_kernel(wga_ref, wup_ref, o_ref):
    o_ref[...] = jnp.dot(wga_ref[...], wup_ref[...], preferred_element_type=F32,
                         precision=lax.Precision.HIGHEST)


def _decay_weight(w_ga, w_up):
    D = w_ga.shape[0]
    return pl.pallas_call(
        _decay_weight_kernel,
        out_shape=jax.ShapeDtypeStruct((D, GLA_QK), F32),
        name="decay_weight",
    )(w_ga, w_up)


def _inproj_kernel(x_ref, mod_ref, w_ref, cs_ref, hf_ref, hb_ref, u_ref):
    j = pl.program_id(1)

    @pl.when(j == 0)
    def _():
        shift = mod_ref[0, :, 0 * D_MODEL:1 * D_MODEL]
        scale1p = 1.0 + mod_ref[0, :, 1 * D_MODEL:2 * D_MODEL]

        def rows(t, carry):
            r0 = pl.multiple_of(t * LN_ROWS, LN_ROWS)
            u = _ln(x_ref[pl.ds(r0, LN_ROWS), :]) * scale1p + shift
            u_ref[pl.ds(r0, LN_ROWS), :] = u.astype(BF16)
            return carry

        lax.fori_loop(0, x_ref.shape[0] // LN_ROWS, rows, 0)

    acc = jnp.dot(u_ref[...], w_ref[...], preferred_element_type=F32) * cs_ref[...]

    @pl.when(j < HF_TILES)
    def _():
        hf_ref[...] = acc

    @pl.when(j >= HF_TILES)
    def _():
        hb_ref[...] = acc.astype(BF16)


def _in_proj(x2d, mod3, w_all, colscale, seq):
    M, D = x2d.shape
    tm = 2048
    per_b = seq // tm
    n_tiles = HF_TILES + HB_TILES
    return pl.pallas_call(
        _inproj_kernel,
        grid=(M // tm, n_tiles),
        in_specs=[pl.BlockSpec((tm, D), lambda i, j: (i, 0)),
                  pl.BlockSpec((1, 1, 6 * D_MODEL), lambda i, j: (i // per_b, 0, 0)),
                  pl.BlockSpec((D, IN_TN), lambda i, j: (0, j)),
                  pl.BlockSpec((1, IN_TN), lambda i, j: (0, j))],
        out_specs=[pl.BlockSpec((tm, IN_TN), lambda i, j: (i, jnp.minimum(j, HF_TILES - 1))),
                   pl.BlockSpec((tm, IN_TN), lambda i, j: (i, jnp.maximum(j - HF_TILES, 0)))],
        out_shape=[jax.ShapeDtypeStruct((M, HF_W), F32),
                   jax.ShapeDtypeStruct((M, HB_W), BF16)],
        scratch_shapes=[pltpu.VMEM((tm, D), BF16)],
        compiler_params=pltpu.CompilerParams(
            dimension_semantics=("arbitrary", "arbitrary"), vmem_limit_bytes=VMEM_LIMIT),
        name="in_proj",
    )(x2d, mod3, w_all, colscale)


GLA_LEVELS = (32, 16, 8, 4, 2, 1)


def _gla_cum_matrix():
    C = GLA_CHUNK
    shape = (C * (1 + len(GLA_LEVELS)), 3 * C)
    r = lax.broadcasted_iota(jnp.int32, shape, 0)
    s = lax.broadcasted_iota(jnp.int32, shape, 1) % C
    lvl = r // C
    rr = r % C
    anchor = rr
    for li, blk in enumerate(GLA_LEVELS):
        anchor = jnp.where(lvl == li + 1, (rr // (2 * blk)) * (2 * blk) + blk - 1, anchor)
    return jnp.where(s <= anchor, 1.0, 0.0).astype(BF16)


def _gla_kernel(q_ref, k_ref, xa_ref, v_ref, gg_ref, ba_ref, gn_ref, o_ref, state_ref, *, ts):
    C = GLA_CHUNK
    si = pl.program_id(1)

    @pl.when(si == 0)
    def _():
        state_ref[...] = jnp.zeros_like(state_ref)

    cum3 = _gla_cum_matrix()
    r = lax.broadcasted_iota(jnp.int32, (C, C), 0)
    s = lax.broadcasted_iota(jnp.int32, (C, C), 1)
    level_masks = [((r // blk) == (s // blk) + 1) & ((r // (2 * blk)) == (s // (2 * blk)))
                   for blk in GLA_LEVELS]
    diag_mask = r == s

    def chunk(ci, carry):
        r0 = pl.multiple_of(ci * C, C)
        la = _log_sigmoid(xa_ref[0, pl.ds(r0, C), :] + ba_ref[...]) * (1.0 / GLA_TAU)
        la_hi = la.astype(BF16)
        rem = la - la_hi.astype(F32)
        la_mid = rem.astype(BF16)
        la_lo = (rem - la_mid.astype(F32)).astype(BF16)
        la3 = jnp.concatenate([la_hi, la_mid, la_lo], axis=0)
        ball = jnp.dot(cum3, la3, preferred_element_type=F32)
        q = q_ref[0, pl.ds(r0, C), :]
        k = k_ref[0, pl.ds(r0, C), :]
        v = v_ref[0, pl.ds(r0, C), :]
        gg = gg_ref[0, pl.ds(r0, C), :].astype(F32)
        for h in range(GLA_HEADS):
            ks = slice(h * GLA_DK, (h + 1) * GLA_DK)
            vs = slice(h * GLA_DV, (h + 1) * GLA_DV)
            b = ball[0:C, ks]
            qh, kh, vh = q[:, ks], k[:, ks], v[:, vs]
            st = state_ref[h]
            o = lax.dot_general((qh * jnp.exp(b)).astype(BF16), st.astype(BF16),
                                (((1,), (1,)), ((), ())),
                                preferred_element_type=F32)
            sc = lax.dot_general(qh.astype(BF16), kh.astype(BF16), (((1,), (1,)), ((), ())),
                                 preferred_element_type=F32)
            p = jnp.where(diag_mask, sc, 0.0)
            for li in range(len(GLA_LEVELS)):
                anc = ball[(li + 1) * C:(li + 2) * C, ks]
                f = jnp.exp(-jnp.abs(b - anc))
                sc = lax.dot_general((qh * f).astype(BF16), (kh * f).astype(BF16),
                                     (((1,), (1,)), ((), ())), preferred_element_type=F32)
                p = jnp.where(level_masks[li], sc, p)
            o = o + jnp.dot(p.astype(BF16), vh, preferred_element_type=F32)
            b_last = b[C - 1:C, :]
            kd = (kh * jnp.exp(b_last - b)).astype(BF16)
            upd = lax.dot_general(vh, kd, (((0,), (0,)), ((), ())),
                                  preferred_element_type=F32)
            state_ref[h] = st * jnp.exp(b_last) + upd
            on = o * lax.rsqrt(jnp.mean(o * o, axis=-1, keepdims=True) + LN_EPS)
            g = gg[:, vs]
            on = on * gn_ref[:, vs] * (g * _sigmoid(g))
            o_ref[0, pl.ds(r0, C), vs] = on.astype(BF16)
        return carry

    lax.fori_loop(0, ts // C, chunk, 0)


def _gla(hf3, hb3, b_alpha, gla_norm_g):
    B, S, _ = hf3.shape
    ts = 512
    kern = functools.partial(_gla_kernel, ts=ts)
    return pl.pallas_call(
        kern,
        grid=(B, S // ts),
        in_specs=[pl.BlockSpec((1, ts, GLA_QK), lambda b, s: (b, s, 0)),
                  pl.BlockSpec((1, ts, GLA_QK), lambda b, s: (b, s, 1)),
                  pl.BlockSpec((1, ts, GLA_QK), lambda b, s: (b, s, 2)),
                  pl.BlockSpec((1, ts, GLA_VW), lambda b, s: (b, s, HB_GV)),
                  pl.BlockSpec((1, ts, GLA_VW), lambda b, s: (b, s, HB_GG)),
                  pl.BlockSpec((1, GLA_QK), lambda b, s: (0, 0)),
                  pl.BlockSpec((1, GLA_VW), lambda b, s: (0, 0))],
        out_specs=pl.BlockSpec((1, ts, GLA_VW), lambda b, s: (b, s, 0)),
        out_shape=jax.ShapeDtypeStruct((B, S, GLA_VW), BF16),
        scratch_shapes=[pltpu.VMEM((GLA_HEADS, GLA_DV, GLA_DK), F32)],
        compiler_params=pltpu.CompilerParams(
            dimension_semantics=("arbitrary", "arbitrary"), vmem_limit_bytes=VMEM_LIMIT),
        name="gla",
    )(hf3, hf3, hf3, hb3, hb3, b_alpha.reshape(1, GLA_QK), gla_norm_g.reshape(1, GLA_VW))


SB_TQ = 256
SB_HALF = 128
SB_DEAD = -104.0


def _sb_kernel(q_ref, k_ref, v_ref, o_ref):
    T, Hk = SB_TQ, SB_HALF
    qi = pl.program_id(2)
    q = q_ref[0]

    jj = lax.broadcasted_iota(jnp.int32, (2 * Hk, 2 * Hk), 0) % Hk
    ss = lax.broadcasted_iota(jnp.int32, (2 * Hk, 2 * Hk), 1)
    mo = jnp.where((ss >= Hk) | (jj > ss), 1.0, 0.0).astype(BF16)
    row = lax.broadcasted_iota(jnp.int32, (T, T), 0)
    col = lax.broadcasted_iota(jnp.int32, (T, T), 1)
    causal = col < row

    def tile(k0, masked, carry, acc):
        k = k_ref[0, pl.ds(k0, T), :]
        v = v_ref[0, pl.ds(k0, T), :]
        z = lax.dot_general(q, k, (((1,), (1,)), ((), ())), preferred_element_type=F32)
        ls = jnp.minimum(z, 0.0) - jnp.log(1.0 + jnp.exp(-jnp.abs(z)))
        lk = ls - z
        if masked:
            lk = jnp.where(causal, lk, 0.0)
        ws = [None, None]
        for half in (1, 0):
            cs = slice(half * Hk, (half + 1) * Hk)
            lkh = lk[:, cs]
            hi = lkh.astype(BF16)
            lo = (lkh - hi.astype(F32)).astype(BF16)
            r = jnp.dot(jnp.concatenate([hi, lo], axis=1), mo, preferred_element_type=F32)
            after = r[:, :Hk] + carry
            carry = carry + r[:, Hk:]
            w = jnp.exp(ls[:, cs] + after)
            if masked:
                w = jnp.where(causal[:, cs], w, 0.0)
            ws[half] = w.astype(BF16)
        acc = acc + jnp.dot(jnp.concatenate(ws, axis=1), v, preferred_element_type=F32)
        return carry, acc

    zero = jnp.zeros((T, Hk), F32)
    carry, acc = tile(pl.multiple_of(qi * T, T), True, zero, zero)

    def cond(st):
        kt, carry, _ = st
        return jnp.logical_and(kt >= 0, jnp.max(carry) > SB_DEAD)

    def body(st):
        kt, carry, acc = st
        carry, acc = tile(pl.multiple_of(kt * T, T), False, carry, acc)
        return kt - 1, carry, acc

    _, _, acc = lax.while_loop(cond, body, (qi - 1, carry, acc))
    o_ref[0] = acc.astype(BF16)


def _stickbreak(hb3):
    B, S, _ = hb3.shape
    lanes = D_MODEL // SB_DH
    return pl.pallas_call(
        _sb_kernel,
        grid=(B, SB_HEADS, S // SB_TQ),
        in_specs=[pl.BlockSpec((1, SB_TQ, SB_DH), lambda b, h, i: (b, i, HB_SQ * lanes + h)),
                  pl.BlockSpec((1, S, SB_DH), lambda b, h, i: (b, 0, HB_SK * lanes + h)),
                  pl.BlockSpec((1, S, SB_DH), lambda b, h, i: (b, 0, HB_SV * lanes + h))],
        out_specs=pl.BlockSpec((1, SB_TQ, SB_DH), lambda b, h, i: (b, i, h)),
        out_shape=jax.ShapeDtypeStruct((B, S, SB_HEADS * SB_DH), BF16),
        compiler_params=pltpu.CompilerParams(
            dimension_semantics=("arbitrary", "arbitrary", "arbitrary"),
            vmem_limit_bytes=VMEM_LIMIT),
        name="stickbreak",
    )(hb3, hb3, hb3)


def _merge_kernel(x_ref, og_ref, os_ref, ma_ref, mb_ref, mod_ref, w_ref, g_ref, b_ref, o_ref):
    merged = (_sigmoid(ma_ref[...].astype(F32)) * og_ref[...].astype(F32)
              + _sigmoid(mb_ref[...].astype(F32)) * os_ref[...].astype(F32))
    y = jnp.dot(merged.astype(BF16), w_ref[...], preferred_element_type=F32)
    gate = mod_ref[0, :, 2 * D_MODEL:3 * D_MODEL]
    o_ref[...] = _ln(ALPHA * x_ref[...] + (1.0 + gate) * y) * g_ref[...] + b_ref[...]


def _merge_out(x2d, og2, os2, hb2, mod3, w_out, ln_g, ln_b, seq):
    M, D = x2d.shape
    tm = 512
    per_b = seq // tm
    row = lambda i: (i, 0)
    return pl.pallas_call(
        _merge_kernel,
        grid=(M // tm,),
        in_specs=[pl.BlockSpec((tm, D), row),
                  pl.BlockSpec((tm, D), row),
                  pl.BlockSpec((tm, D), row),
                  pl.BlockSpec((tm, D), lambda i: (i, HB_MA)),
                  pl.BlockSpec((tm, D), lambda i: (i, HB_MB)),
                  pl.BlockSpec((1, 1, 6 * D_MODEL), lambda i: (i // per_b, 0, 0)),
                  pl.BlockSpec((D, D), lambda i: (0, 0)),
                  pl.BlockSpec((1, D), lambda i: (0, 0)),
                  pl.BlockSpec((1, D), lambda i: (0, 0))],
        out_specs=pl.BlockSpec((tm, D), row),
        out_shape=jax.ShapeDtypeStruct((M, D), F32),
        compiler_params=pltpu.CompilerParams(
            dimension_semantics=("arbitrary",), vmem_limit_bytes=VMEM_LIMIT),
        name="merge_out",
    )(x2d, og2, os2, hb2, hb2, mod3, w_out, ln_g.reshape(1, D), ln_b.reshape(1, D))


FFN_TF = 1408


def _ffn_kernel(x_ref, mod_ref, wg_ref, wu_ref, wo_ref, g_ref, b_ref, o_ref, u_ref, acc_ref):
    j = pl.program_id(1)

    @pl.when(j == 0)
    def _():
        shift = mod_ref[0, :, 3 * D_MODEL:4 * D_MODEL]
        scale = mod_ref[0, :, 4 * D_MODEL:5 * D_MODEL]
        u_ref[...] = (_ln(x_ref[...]) * (1.0 + scale) + shift).astype(BF16)

    u = u_ref[...]
    g = jnp.dot(u, wg_ref[...], preferred_element_type=F32)
    up = jnp.dot(u, wu_ref[...], preferred_element_type=F32)
    a = (g * _sigmoid(g) * up).astype(BF16)
    part = jnp.dot(a, wo_ref[...], preferred_element_type=F32)

    @pl.when(j == 0)
    def _():
        acc_ref[...] = part

    @pl.when(j == pl.num_programs(1) - 1)
    def _():
        f = acc_ref[...] + part
        gate = mod_ref[0, :, 5 * D_MODEL:6 * D_MODEL]
        o_ref[...] = _ln(ALPHA * x_ref[...] + (1.0 + gate) * f) * g_ref[...] + b_ref[...]


def _ffn(x2d, mod3, w_in, w_out, ln_g, ln_b, seq):
    M, D = x2d.shape
    tm = 512
    per_b = seq // tm
    nf = D_FF // FFN_TF
    assert nf == 2
    return pl.pallas_call(
        _ffn_kernel,
        grid=(M // tm, nf),
        in_specs=[pl.BlockSpec((tm, D), lambda i, j: (i, 0)),
                  pl.BlockSpec((1, 1, 6 * D_MODEL), lambda i, j: (i // per_b, 0, 0)),
                  pl.BlockSpec((D, FFN_TF), lambda i, j: (0, j)),
                  pl.BlockSpec((D, FFN_TF), lambda i, j: (0, nf + j)),
                  pl.BlockSpec((FFN_TF, D), lambda i, j: (j, 0)),
                  pl.BlockSpec((1, D), lambda i, j: (0, 0)),
                  pl.BlockSpec((1, D), lambda i, j: (0, 0))],
        out_specs=pl.BlockSpec((tm, D), lambda i, j: (i, 0)),
        out_shape=jax.ShapeDtypeStruct((M, D), F32),
        scratch_shapes=[pltpu.VMEM((tm, D), BF16), pltpu.VMEM((tm, D), F32)],
        compiler_params=pltpu.CompilerParams(
            dimension_semantics=("arbitrary", "arbitrary"), vmem_limit_bytes=VMEM_LIMIT),
        name="ffn",
    )(x2d, mod3, w_in, w_in, w_out, ln_g.reshape(1, D), ln_b.reshape(1, D))


def _regroup_w_in(w_in_l, w_decay):
    o = 0
    parts = {}
    for name, wdt in (("gq", GLA_QK), ("gk", GLA_QK), ("gv", GLA_VW), ("gg", GLA_VW),
                      ("ga", GLA_LOWRANK), ("sq", D_MODEL), ("sk", D_MODEL), ("sv", D_MODEL),
                      ("ma", D_MODEL), ("mb", D_MODEL)):
        parts[name] = w_in_l[:, o:o + wdt]
        o += wdt
    w_all = jnp.concatenate(
        [parts["gq"], parts["gk"], w_decay, parts["gv"], parts["gg"], parts["sq"],
         parts["sk"], parts["sv"], parts["ma"], parts["mb"]], axis=1).astype(BF16)
    return w_all, parts["ga"]


def _colscale():
    ones = lambda n: jnp.ones((n,), F32)
    return jnp.concatenate([
        ones(GLA_QK) * (GLA_DK ** -0.5), ones(2 * GLA_QK), ones(2 * GLA_VW),
        ones(D_MODEL) * (SB_DH ** -0.5), ones(4 * D_MODEL)]).reshape(1, HF_W + HB_W)


def kernel(x, c, w_ada, b_ada, w_in, w_alpha_up, b_alpha, gla_norm_g, w_out,
           ln1_g, ln1_b, w_ffn_in, w_ffn_out, ln2_g, ln2_b):
    B, S, D = x.shape
    M = B * S
    x2d = x.reshape(M, D)
    for l in range(DEPTH):
        mod3 = _adaln_mod(c, w_ada[l], b_ada[l]).reshape(B, 1, 6 * D)
        ga_off = 2 * GLA_QK + 2 * GLA_VW
        w_decay = _decay_weight(w_in[l][:, ga_off:ga_off + GLA_LOWRANK], w_alpha_up[l])
        w_all, _ = _regroup_w_in(w_in[l], w_decay)
        hf, hb = _in_proj(x2d, mod3, w_all, _colscale(), S)
        hf3 = hf.reshape(B, S, HF_W)
        hb3 = hb.reshape(B, S, HB_W)
        og = _gla(hf3, hb3, b_alpha[l], gla_norm_g[l])
        osb = _stickbreak(hb3)
        x2d = _merge_out(x2d, og.reshape(M, D), osb.reshape(M, D), hb, mod3,
                         w_out[l].astype(BF16), ln1_g[l], ln1_b[l], S)
        x2d = _ffn(x2d, mod3, w_ffn_in[l].astype(BF16), w_ffn_out[l].astype(BF16),
                   ln2_g[l], ln2_b[l], S)
    return x2d.reshape(B, S, D)
```

```python
import functools
import math

import numpy as np
import jax
import jax.numpy as jnp
from jax import lax
from jax.experimental import pallas as pl
from jax.experimental.pallas import tpu as pltpu

F32 = jnp.float32
BF16 = jnp.bfloat16

D_MODEL = 1024
GLA_HEADS = 4
GLA_DK = 128
GLA_DV = 256
GLA_QK = GLA_HEADS * GLA_DK
GLA_VW = GLA_HEADS * GLA_DV
GLA_LOWRANK = 16
GLA_TAU = 16.0
GLA_CHUNK = 64
SB_HEADS = 8
SB_DH = 128
D_FF = 2816
DEPTH = 1
ALPHA = (2.0 * DEPTH) ** 0.25
LN_EPS = 1e-5
LOG2E = math.log2(math.e)

HF_W = 3 * GLA_QK
HB_W = 7 * D_MODEL
IN_TN = 512
HF_TILES = HF_W // IN_TN
HB_TILES = HB_W // IN_TN
HB_GV, HB_GG, HB_SQ, HB_SK, HB_SV, HB_MA, HB_MB = 0, 1, 2, 3, 4, 5, 6

VMEM_LIMIT = 56 * 1024 * 1024
LN_ROWS = 256


def _ln(x):
    mu = jnp.mean(x, axis=-1, keepdims=True)
    xc = x - mu
    var = jnp.mean(xc * xc, axis=-1, keepdims=True)
    return xc * lax.rsqrt(var + LN_EPS)


def _sigmoid(x):
    return 1.0 / (1.0 + jnp.exp(-x))


def _log2_sigmoid2(x2):
    return jnp.minimum(x2, 0.0) - jnp.log(1.0 + jnp.exp2(-jnp.abs(x2))) * LOG2E


def _adaln_kernel(c_ref, w_ref, b_ref, o_ref):
    c = c_ref[...]
    ca = c * _sigmoid(c)
    o_ref[...] = jnp.dot(ca, w_ref[...], preferred_element_type=F32,
                         precision=lax.Precision.HIGHEST) + b_ref[...]


def _adaln_mod(c, w_ada, b_ada):
    B, D = c.shape
    N = w_ada.shape[1]
    tn = 1024
    return pl.pallas_call(
        _adaln_kernel,
        grid=(N // tn,),
        in_specs=[pl.BlockSpec((B, D), lambda j: (0, 0)),
                  pl.BlockSpec((D, tn), lambda j: (0, j)),
                  pl.BlockSpec((1, tn), lambda j: (0, j))],
        out_specs=pl.BlockSpec((B, tn), lambda j: (0, j)),
        out_shape=jax.ShapeDtypeStruct((B, N), F32),
        name="adaln_mod",
    )(c, w_ada, b_ada.reshape(1, N))


def _decay_weight_kernel(wga_ref, wup_ref, o_ref):
    o_ref[...] = jnp.dot(wga_ref[...], wup_ref[...], preferred_element_type=F32,
                         precision=lax.Precision.HIGHEST)


def _decay_weight(w_ga, w_up):
    D = w_ga.shape[0]
    return pl.pallas_call(
        _decay_weight_kernel,
        out_shape=jax.ShapeDtypeStruct((D, GLA_QK), F32),
        name="decay_weight",
    )(w_ga, w_up)


def _inproj_kernel(x_ref, mod_ref, w_ref, cs_ref, hf_ref, hb_ref, u_ref):
    j = pl.program_id(1)

    @pl.when(j == 0)
    def _():
        shift = mod_ref[0, :, 0 * D_MODEL:1 * D_MODEL]
        scale1p = 1.0 + mod_ref[0, :, 1 * D_MODEL:2 * D_MODEL]

        def rows(t, carry):
            r0 = pl.multiple_of(t * LN_ROWS, LN_ROWS)
            u = _ln(x_ref[pl.ds(r0, LN_ROWS), :]) * scale1p + shift
            u_ref[pl.ds(r0, LN_ROWS), :] = u.astype(BF16)
            return carry

        lax.fori_loop(0, x_ref.shape[0] // LN_ROWS, rows, 0)

    acc = jnp.dot(u_ref[...], w_ref[...], preferred_element_type=F32) * cs_ref[...]

    @pl.when(j < HF_TILES)
    def _():
        hf_ref[...] = acc

    @pl.when(j >= HF_TILES)
    def _():
        hb_ref[...] = acc.astype(BF16)


def _in_proj(x2d, mod3, w_all, colscale, seq):
    M, D = x2d.shape
    tm = 2048
    per_b = seq // tm
    n_tiles = HF_TILES + HB_TILES
    return pl.pallas_call(
        _inproj_kernel,
        grid=(M // tm, n_tiles),
        in_specs=[pl.BlockSpec((tm, D), lambda i, j: (i, 0)),
                  pl.BlockSpec((1, 1, 6 * D_MODEL), lambda i, j: (i // per_b, 0, 0)),
                  pl.BlockSpec((D, IN_TN), lambda i, j: (0, j)),
                  pl.BlockSpec((1, IN_TN), lambda i, j: (0, j))],
        out_specs=[pl.BlockSpec((tm, IN_TN), lambda i, j: (i, jnp.minimum(j, HF_TILES - 1))),
                   pl.BlockSpec((tm, IN_TN), lambda i, j: (i, jnp.maximum(j - HF_TILES, 0)))],
        out_shape=[jax.ShapeDtypeStruct((M, HF_W), F32),
                   jax.ShapeDtypeStruct((M, HB_W), BF16)],
        scratch_shapes=[pltpu.VMEM((tm, D), BF16)],
        compiler_params=pltpu.CompilerParams(
            dimension_semantics=("arbitrary", "arbitrary"), vmem_limit_bytes=VMEM_LIMIT),
        name="in_proj",
    )(x2d, mod3, w_all, colscale)


GLA_LEVELS = (32, 16, 8, 4, 2, 1)


def _gla_cum_matrix():
    C = GLA_CHUNK
    r = np.arange(C)[:, None]
    s = np.arange(C)[None, :]
    blocks = [s <= r]
    for blk in GLA_LEVELS:
        blocks.append(s <= (r // (2 * blk)) * (2 * blk) + blk - 1)
    one = np.concatenate(blocks, axis=0).astype(np.float32)
    return np.concatenate([one, one, one], axis=1)


def _gla_pair_masks():
    C = GLA_CHUNK
    r = np.arange(C)[:, None]
    s = np.arange(C)[None, :]
    masks = [r == s]
    for blk in GLA_LEVELS:
        masks.append(((r // blk) == (s // blk) + 1) & ((r // (2 * blk)) == (s // (2 * blk))))
    return np.stack(masks).astype(np.float32)


def _gla_kernel(q_ref, k_ref, xa_ref, v_ref, gg_ref, ba_ref, gn_ref, pm_ref, cum_ref, o_ref,
                state_ref, *, ts):
    C = GLA_CHUNK
    si = pl.program_id(1)

    @pl.when(si == 0)
    def _():
        state_ref[...] = jnp.zeros_like(state_ref)

    ba2 = ba_ref[...] * LOG2E

    def chunk(ci, carry):
        r0 = pl.multiple_of(ci * C, C)
        la = _log2_sigmoid2(xa_ref[0, pl.ds(r0, C), :] + ba2) * (1.0 / GLA_TAU)
        la_hi = la.astype(BF16)
        rem = la - la_hi.astype(F32)
        la_mid = rem.astype(BF16)
        la_lo = (rem - la_mid.astype(F32)).astype(BF16)
        la3 = jnp.concatenate([la_hi, la_mid, la_lo], axis=0)
        ball = jnp.dot(cum_ref[...], la3, preferred_element_type=F32)
        q = q_ref[0, pl.ds(r0, C), :]
        k = k_ref[0, pl.ds(r0, C), :]
        v = v_ref[0, pl.ds(r0, C), :]
        gg = gg_ref[0, pl.ds(r0, C), :].astype(F32)
        for h in range(GLA_HEADS):
            ks = slice(h * GLA_DK, (h + 1) * GLA_DK)
            vs = slice(h * GLA_DV, (h + 1) * GLA_DV)
            b = ball[0:C, ks]
            qh, kh, vh = q[:, ks], k[:, ks], v[:, vs]
            st = state_ref[h]
            o = lax.dot_general((qh * jnp.exp2(b)).astype(BF16), st.astype(BF16),
                                (((1,), (1,)), ((), ())),
                                preferred_element_type=F32)
            sc = lax.dot_general(qh.astype(BF16), kh.astype(BF16), (((1,), (1,)), ((), ())),
                                 preferred_element_type=F32)
            p = jnp.where(pm_ref[0] != 0.0, sc, 0.0)
            for li in range(len(GLA_LEVELS)):
                anc = ball[(li + 1) * C:(li + 2) * C, ks]
                f = jnp.exp2(-jnp.abs(b - anc))
                sc = lax.dot_general((qh * f).astype(BF16), (kh * f).astype(BF16),
                                     (((1,), (1,)), ((), ())), preferred_element_type=F32)
                p = jnp.where(pm_ref[li + 1] != 0.0, sc, p)
            o = o + jnp.dot(p.astype(BF16), vh, preferred_element_type=F32)
            b_last = b[C - 1:C, :]
            kd = (kh * jnp.exp2(b_last - b)).astype(BF16)
            upd = lax.dot_general(vh, kd, (((0,), (0,)), ((), ())),
                                  preferred_element_type=F32)
            state_ref[h] = st * jnp.exp2(b_last) + upd
            on = o * lax.rsqrt(jnp.mean(o * o, axis=-1, keepdims=True) + LN_EPS)
            g = gg[:, vs]
            on = on * gn_ref[:, vs] * (g * _sigmoid(g))
            o_ref[0, pl.ds(r0, C), vs] = on.astype(BF16)
        return carry

    lax.fori_loop(0, ts // C, chunk, 0, unroll=2)


def _gla(hf3, hb3, b_alpha, gla_norm_g):
    B, S, _ = hf3.shape
    ts = 512
    kern = functools.partial(_gla_kernel, ts=ts)
    n_masks = 1 + len(GLA_LEVELS)
    return pl.pallas_call(
        kern,
        grid=(B, S // ts),
        in_specs=[pl.BlockSpec((1, ts, GLA_QK), lambda b, s: (b, s, 0)),
                  pl.BlockSpec((1, ts, GLA_QK), lambda b, s: (b, s, 1)),
                  pl.BlockSpec((1, ts, GLA_QK), lambda b, s: (b, s, 2)),
                  pl.BlockSpec((1, ts, GLA_VW), lambda b, s: (b, s, HB_GV)),
                  pl.BlockSpec((1, ts, GLA_VW), lambda b, s: (b, s, HB_GG)),
                  pl.BlockSpec((1, GLA_QK), lambda b, s: (0, 0)),
                  pl.BlockSpec((1, GLA_VW), lambda b, s: (0, 0)),
                  pl.BlockSpec((n_masks, GLA_CHUNK, GLA_CHUNK), lambda b, s: (0, 0, 0)),
                  pl.BlockSpec((n_masks * GLA_CHUNK, 3 * GLA_CHUNK), lambda b, s: (0, 0))],
        out_specs=pl.BlockSpec((1, ts, GLA_VW), lambda b, s: (b, s, 0)),
        out_shape=jax.ShapeDtypeStruct((B, S, GLA_VW), BF16),
        scratch_shapes=[pltpu.VMEM((GLA_HEADS, GLA_DV, GLA_DK), F32)],
        compiler_params=pltpu.CompilerParams(
            dimension_semantics=("arbitrary", "arbitrary"), vmem_limit_bytes=VMEM_LIMIT),
        name="gla",
    )(hf3, hf3, hf3, hb3, hb3, b_alpha.reshape(1, GLA_QK), gla_norm_g.reshape(1, GLA_VW),
      jnp.asarray(_gla_pair_masks()), jnp.asarray(_gla_cum_matrix(), dtype=BF16))


SB_TQ = 256
SB_SUB = 128
SB_HP = 4
SB_MASKED = -1e30
SB_DEAD2 = -152.0


def _sb_suffix_matrix():
    j = np.arange(2 * SB_SUB)[:, None] % SB_SUB
    s = np.arange(2 * SB_SUB)[None, :]
    return ((s >= SB_SUB) | (j > s)).astype(np.float32)


def _sb_causal_bias():
    r = np.arange(SB_TQ)[:, None]
    c = np.arange(SB_TQ)[None, :]
    return np.where(c < r, 0.0, SB_MASKED).astype(np.float32)


def _sb_scores(q, k):
    return lax.dot_general(q, k, (((1,), (1,)), ((), ())), preferred_element_type=F32)


def _sb_window(z, v, carry, mo):
    n_sub = z.shape[1] // SB_SUB
    ws = [None] * n_sub
    for c in reversed(range(n_sub)):
        zc = z[:, c * SB_SUB:(c + 1) * SB_SUB]
        ls = _log2_sigmoid2(zc)
        lk = ls - zc
        hi = lk.astype(BF16)
        lo = (lk - hi.astype(F32)).astype(BF16)
        r = jnp.dot(jnp.concatenate([hi, lo], axis=1), mo, preferred_element_type=F32)
        ws[c] = jnp.exp2(ls + (r[:, :SB_SUB] + carry)).astype(BF16)
        carry = carry + r[:, SB_SUB:]
    return carry, jnp.dot(jnp.concatenate(ws, axis=1), v, preferred_element_type=F32)


def _sb_kernel(q_ref, k_ref, v_ref, mo_ref, bias_ref, o_ref):
    T = SB_TQ
    n_q = q_ref.shape[1] // T
    heads = [slice(h * SB_DH, (h + 1) * SB_DH) for h in range(SB_HP)]
    zero = jnp.zeros((T, SB_SUB), F32)

    def finish(q0, qs, carries, accs):
        def cond(st):
            k0, carries, _ = st
            live = functools.reduce(jnp.maximum, carries)
            return jnp.logical_and(k0 >= 0, jnp.max(live) > SB_DEAD2)

        def body(st):
            k0, carries, accs = st
            k0 = pl.multiple_of(k0, T)
            new_c, new_a = [], []
            for h, hs in enumerate(heads):
                z = _sb_scores(qs[h], k_ref[0, pl.ds(k0, T), hs])
                c, pv = _sb_window(z, v_ref[0, pl.ds(k0, T), hs], carries[h], mo_ref[...])
                new_c.append(c)
                new_a.append(accs[h] + pv)
            return k0 - T, tuple(new_c), tuple(new_a)

        _, _, accs = lax.while_loop(cond, body, (q0 - 2 * T, tuple(carries), tuple(accs)))
        for h, hs in enumerate(heads):
            o_ref[0, pl.ds(q0, T), hs] = accs[h].astype(BF16)

    for hs in heads:
        z = _sb_scores(q_ref[0, 0:T, hs], k_ref[0, 0:T, hs]) + bias_ref[...]
        _, acc = _sb_window(z, v_ref[0, 0:T, hs], zero, mo_ref[...])
        o_ref[0, 0:T, hs] = acc.astype(BF16)

    def step(i, carry):
        q0 = pl.multiple_of(i * T, T)
        w0 = pl.multiple_of(q0 - T, T)
        qs, carries, accs = [], [], []
        for hs in heads:
            q = q_ref[0, pl.ds(q0, T), hs]
            z = _sb_scores(q, k_ref[0, pl.ds(w0, 2 * T), hs])
            z = jnp.concatenate([z[:, :T], z[:, T:] + bias_ref[...]], axis=1)
            c, acc = _sb_window(z, v_ref[0, pl.ds(w0, 2 * T), hs], zero, mo_ref[...])
            qs.append(q)
            carries.append(c)
            accs.append(acc)
        finish(q0, qs, carries, accs)
        return carry

    lax.fori_loop(1, n_q, step, 0)


def _stickbreak(hb3):
    B, S, _ = hb3.shape
    wblk = SB_HP * SB_DH
    per_group = D_MODEL // wblk
    blk = lambda group: pl.BlockSpec((1, S, wblk), lambda b, g: (b, 0, group * per_group + g))
    const = lambda a: pl.BlockSpec(a.shape, lambda b, g: (0, 0))
    mo = jnp.asarray(_sb_suffix_matrix(), dtype=BF16)
    bias = jnp.asarray(_sb_causal_bias())
    return pl.pallas_call(
        _sb_kernel,
        grid=(B, SB_HEADS // SB_HP),
        in_specs=[blk(HB_SQ), blk(HB_SK), blk(HB_SV), const(mo), const(bias)],
        out_specs=pl.BlockSpec((1, S, wblk), lambda b, g: (b, 0, g)),
        out_shape=jax.ShapeDtypeStruct((B, S, SB_HEADS * SB_DH), BF16),
        compiler_params=pltpu.CompilerParams(
            dimension_semantics=("arbitrary", "arbitrary"), vmem_limit_bytes=VMEM_LIMIT),
        name="stickbreak",
    )(hb3, hb3, hb3, mo, bias)


def _merge_kernel(x_ref, og_ref, os_ref, ma_ref, mb_ref, mod_ref, w_ref, g_ref, b_ref, o_ref):
    merged = (_sigmoid(ma_ref[...].astype(F32)) * og_ref[...].astype(F32)
              + _sigmoid(mb_ref[...].astype(F32)) * os_ref[...].astype(F32))
    y = jnp.dot(merged.astype(BF16), w_ref[...], preferred_element_type=F32)
    gate = mod_ref[0, :, 2 * D_MODEL:3 * D_MODEL]
    o_ref[...] = _ln(ALPHA * x_ref[...] + (1.0 + gate) * y) * g_ref[...] + b_ref[...]


def _merge_out(x2d, og2, os2, hb2, mod3, w_out, ln_g, ln_b, seq):
    M, D = x2d.shape
    tm = 512
    per_b = seq // tm
    row = lambda i: (i, 0)
    return pl.pallas_call(
        _merge_kernel,
        grid=(M // tm,),
        in_specs=[pl.BlockSpec((tm, D), row),
                  pl.BlockSpec((tm, D), row),
                  pl.BlockSpec((tm, D), row),
                  pl.BlockSpec((tm, D), lambda i: (i, HB_MA)),
                  pl.BlockSpec((tm, D), lambda i: (i, HB_MB)),
                  pl.BlockSpec((1, 1, 6 * D_MODEL), lambda i: (i // per_b, 0, 0)),
                  pl.BlockSpec((D, D), lambda i: (0, 0)),
                  pl.BlockSpec((1, D), lambda i: (0, 0)),
                  pl.BlockSpec((1, D), lambda i: (0, 0))],
        out_specs=pl.BlockSpec((tm, D), row),
        out_shape=jax.ShapeDtypeStruct((M, D), F32),
        compiler_params=pltpu.CompilerParams(
            dimension_semantics=("arbitrary",), vmem_limit_bytes=VMEM_LIMIT),
        name="merge_out",
    )(x2d, og2, os2, hb2, hb2, mod3, w_out, ln_g.reshape(1, D), ln_b.reshape(1, D))


FFN_TF = 1408


def _ffn_kernel(x_ref, mod_ref, wg_ref, wu_ref, wo_ref, g_ref, b_ref, o_ref, u_ref, acc_ref):
    j = pl.program_id(1)

    @pl.when(j == 0)
    def _():
        shift = mod_ref[0, :, 3 * D_MODEL:4 * D_MODEL]
        scale = mod_ref[0, :, 4 * D_MODEL:5 * D_MODEL]
        u_ref[...] = (_ln(x_ref[...]) * (1.0 + scale) + shift).astype(BF16)

    u = u_ref[...]
    g = jnp.dot(u, wg_ref[...], preferred_element_type=F32)
    up = jnp.dot(u, wu_ref[...], preferred_element_type=F32)
    a = (g * _sigmoid(g) * up).astype(BF16)
    part = jnp.dot(a, wo_ref[...], preferred_element_type=F32)

    @pl.when(j == 0)
    def _():
        acc_ref[...] = part

    @pl.when(j == pl.num_programs(1) - 1)
    def _():
        f = acc_ref[...] + part
        gate = mod_ref[0, :, 5 * D_MODEL:6 * D_MODEL]
        o_ref[...] = _ln(ALPHA * x_ref[...] + (1.0 + gate) * f) * g_ref[...] + b_ref[...]


def _ffn(x2d, mod3, w_in, w_out, ln_g, ln_b, seq):
    M, D = x2d.shape
    tm = 512
    per_b = seq // tm
    nf = D_FF // FFN_TF
    assert nf == 2
    return pl.pallas_call(
        _ffn_kernel,
        grid=(M // tm, nf),
        in_specs=[pl.BlockSpec((tm, D), lambda i, j: (i, 0)),
                  pl.BlockSpec((1, 1, 6 * D_MODEL), lambda i, j: (i // per_b, 0, 0)),
                  pl.BlockSpec((D, FFN_TF), lambda i, j: (0, j)),
                  pl.BlockSpec((D, FFN_TF), lambda i, j: (0, nf + j)),
                  pl.BlockSpec((FFN_TF, D), lambda i, j: (j, 0)),
                  pl.BlockSpec((1, D), lambda i, j: (0, 0)),
                  pl.BlockSpec((1, D), lambda i, j: (0, 0))],
        out_specs=pl.BlockSpec((tm, D), lambda i, j: (i, 0)),
        out_shape=jax.ShapeDtypeStruct((M, D), F32),
        scratch_shapes=[pltpu.VMEM((tm, D), BF16), pltpu.VMEM((tm, D), F32)],
        compiler_params=pltpu.CompilerParams(
            dimension_semantics=("arbitrary", "arbitrary"), vmem_limit_bytes=VMEM_LIMIT),
        name="ffn",
    )(x2d, mod3, w_in, w_in, w_out, ln_g.reshape(1, D), ln_b.reshape(1, D))


def _regroup_w_in(w_in_l, w_decay):
    o = 0
    parts = {}
    for name, wdt in (("gq", GLA_QK), ("gk", GLA_QK), ("gv", GLA_VW), ("gg", GLA_VW),
                      ("ga", GLA_LOWRANK), ("sq", D_MODEL), ("sk", D_MODEL), ("sv", D_MODEL),
                      ("ma", D_MODEL), ("mb", D_MODEL)):
        parts[name] = w_in_l[:, o:o + wdt]
        o += wdt
    w_all = jnp.concatenate(
        [parts["gq"], parts["gk"], w_decay, parts["gv"], parts["gg"], parts["sq"],
         parts["sk"], parts["sv"], parts["ma"], parts["mb"]], axis=1).astype(BF16)
    return w_all, parts["ga"]


def _colscale():
    ones = lambda n: jnp.ones((n,), F32)
    return jnp.concatenate([
        ones(GLA_QK) * (GLA_DK ** -0.5), ones(GLA_QK), ones(GLA_QK) * LOG2E, ones(2 * GLA_VW),
        ones(D_MODEL) * (SB_DH ** -0.5 * LOG2E), ones(4 * D_MODEL)]).reshape(1, HF_W + HB_W)


def kernel(x, c, w_ada, b_ada, w_in, w_alpha_up, b_alpha, gla_norm_g, w_out,
           ln1_g, ln1_b, w_ffn_in, w_ffn_out, ln2_g, ln2_b):
    B, S, D = x.shape
    M = B * S
    x2d = x.reshape(M, D)
    for l in range(DEPTH):
        mod3 = _adaln_mod(c, w_ada[l], b_ada[l]).reshape(B, 1, 6 * D)
        ga_off = 2 * GLA_QK + 2 * GLA_VW
        w_decay = _decay_weight(w_in[l][:, ga_off:ga_off + GLA_LOWRANK], w_alpha_up[l])
        w_all, _ = _regroup_w_in(w_in[l], w_decay)
        hf, hb = _in_proj(x2d, mod3, w_all, _colscale(), S)
        hf3 = hf.reshape(B, S, HF_W)
        hb3 = hb.reshape(B, S, HB_W)
        og = _gla(hf3, hb3, b_alpha[l], gla_norm_g[l])
        osb = _stickbreak(hb3)
        x2d = _merge_out(x2d, og.reshape(M, D), osb.reshape(M, D), hb, mod3,
                         w_out[l].astype(BF16), ln1_g[l], ln1_b[l], S)
        x2d = _ffn(x2d, mod3, w_ffn_in[l].astype(BF16), w_ffn_out[l].astype(BF16),
                   ln2_g[l], ln2_b[l], S)
    return x2d.reshape(B, S, D)
```

```python
import functools
import math

import numpy as np
import jax
import jax.numpy as jnp
from jax import lax
from jax.experimental import pallas as pl
from jax.experimental.pallas import tpu as pltpu

F32 = jnp.float32
BF16 = jnp.bfloat16

D_MODEL = 1024
GLA_HEADS = 4
GLA_DK = 128
GLA_DV = 256
GLA_QK = GLA_HEADS * GLA_DK
GLA_VW = GLA_HEADS * GLA_DV
GLA_LOWRANK = 16
GLA_TAU = 16.0
GLA_CHUNK = 64
SB_HEADS = 8
SB_DH = 128
D_FF = 2816
DEPTH = 1
ALPHA = (2.0 * DEPTH) ** 0.25
LN_EPS = 1e-5
LOG2E = math.log2(math.e)

HF_W = 3 * GLA_QK
HB_W = 7 * D_MODEL
IN_TN = 512
HF_TILES = HF_W // IN_TN
HB_TILES = HB_W // IN_TN
HB_GV, HB_GG, HB_SQ, HB_SK, HB_SV, HB_MA, HB_MB = 0, 1, 2, 3, 4, 5, 6

VMEM_LIMIT = 56 * 1024 * 1024
LN_ROWS = 256


def _ln(x):
    mu = jnp.mean(x, axis=-1, keepdims=True)
    xc = x - mu
    var = jnp.mean(xc * xc, axis=-1, keepdims=True)
    return xc * lax.rsqrt(var + LN_EPS)


def _sigmoid(x):
    return 1.0 / (1.0 + jnp.exp(-x))


def _log2_sigmoid2(x2):
    return jnp.minimum(x2, 0.0) - jnp.log(1.0 + jnp.exp2(-jnp.abs(x2))) * LOG2E


def _adaln_kernel(c_ref, w_ref, b_ref, o_ref):
    c = c_ref[...]
    ca = c * _sigmoid(c)
    o_ref[...] = jnp.dot(ca, w_ref[...], preferred_element_type=F32,
                         precision=lax.Precision.HIGHEST) + b_ref[...]


def _adaln_mod(c, w_ada, b_ada):
    B, D = c.shape
    N = w_ada.shape[1]
    tn = 1024
    return pl.pallas_call(
        _adaln_kernel,
        grid=(N // tn,),
        in_specs=[pl.BlockSpec((B, D), lambda j: (0, 0)),
                  pl.BlockSpec((D, tn), lambda j: (0, j)),
                  pl.BlockSpec((1, tn), lambda j: (0, j))],
        out_specs=pl.BlockSpec((B, tn), lambda j: (0, j)),
        out_shape=jax.ShapeDtypeStruct((B, N), F32),
        name="adaln_mod",
    )(c, w_ada, b_ada.reshape(1, N))


def _decay_weight_kernel(wga_ref, wup_ref, o_ref):
    o_ref[...] = jnp.dot(wga_ref[...], wup_ref[...], preferred_element_type=F32,
                         precision=lax.Precision.HIGHEST)


def _decay_weight(w_ga, w_up):
    D = w_ga.shape[0]
    return pl.pallas_call(
        _decay_weight_kernel,
        out_shape=jax.ShapeDtypeStruct((D, GLA_QK), F32),
        name="decay_weight",
    )(w_ga, w_up)


def _inproj_kernel(x_ref, mod_ref, w_ref, cs_ref, hf_ref, hb_ref, u_ref):
    j = pl.program_id(1)

    @pl.when(j == 0)
    def _():
        shift = mod_ref[0, :, 0 * D_MODEL:1 * D_MODEL]
        scale1p = 1.0 + mod_ref[0, :, 1 * D_MODEL:2 * D_MODEL]

        def rows(t, carry):
            r0 = pl.multiple_of(t * LN_ROWS, LN_ROWS)
            u = _ln(x_ref[pl.ds(r0, LN_ROWS), :]) * scale1p + shift
            u_ref[pl.ds(r0, LN_ROWS), :] = u.astype(BF16)
            return carry

        lax.fori_loop(0, x_ref.shape[0] // LN_ROWS, rows, 0)

    acc = jnp.dot(u_ref[...], w_ref[...], preferred_element_type=F32) * cs_ref[...]

    @pl.when(j < HF_TILES)
    def _():
        hf_ref[...] = acc

    @pl.when(j >= HF_TILES)
    def _():
        hb_ref[...] = acc.astype(BF16)


def _in_proj(x2d, mod3, w_all, colscale, seq):
    M, D = x2d.shape
    tm = 2048
    per_b = seq // tm
    n_tiles = HF_TILES + HB_TILES
    return pl.pallas_call(
        _inproj_kernel,
        grid=(M // tm, n_tiles),
        in_specs=[pl.BlockSpec((tm, D), lambda i, j: (i, 0)),
                  pl.BlockSpec((1, 1, 6 * D_MODEL), lambda i, j: (i // per_b, 0, 0)),
                  pl.BlockSpec((D, IN_TN), lambda i, j: (0, j)),
                  pl.BlockSpec((1, IN_TN), lambda i, j: (0, j))],
        out_specs=[pl.BlockSpec((tm, IN_TN), lambda i, j: (i, jnp.minimum(j, HF_TILES - 1))),
                   pl.BlockSpec((tm, IN_TN), lambda i, j: (i, jnp.maximum(j - HF_TILES, 0)))],
        out_shape=[jax.ShapeDtypeStruct((M, HF_W), F32),
                   jax.ShapeDtypeStruct((M, HB_W), BF16)],
        scratch_shapes=[pltpu.VMEM((tm, D), BF16)],
        compiler_params=pltpu.CompilerParams(
            dimension_semantics=("arbitrary", "arbitrary"), vmem_limit_bytes=VMEM_LIMIT),
        name="in_proj",
    )(x2d, mod3, w_all, colscale)


GLA_LEVELS = (32, 16, 8, 4, 2, 1)


def _gla_cum_matrix():
    C = GLA_CHUNK
    r = np.arange(C)[:, None]
    s = np.arange(C)[None, :]
    blocks = [s <= r]
    for blk in GLA_LEVELS:
        blocks.append(s <= (r // (2 * blk)) * (2 * blk) + blk - 1)
    one = np.concatenate(blocks, axis=0).astype(np.float32)
    return np.concatenate([one, one, one], axis=1)


def _gla_pair_masks():
    C = GLA_CHUNK
    r = np.arange(C)[:, None]
    s = np.arange(C)[None, :]
    masks = [r == s]
    for blk in GLA_LEVELS:
        masks.append(((r // blk) == (s // blk) + 1) & ((r // (2 * blk)) == (s // (2 * blk))))
    return np.stack(masks).astype(np.float32)


def _gla_kernel(q_ref, k_ref, xa_ref, v_ref, gg_ref, ba_ref, gn_ref, pm_ref, cum_ref, o_ref,
                state_ref, *, ts):
    C = GLA_CHUNK
    si = pl.program_id(1)

    @pl.when(si == 0)
    def _():
        state_ref[...] = jnp.zeros_like(state_ref)

    ba2 = ba_ref[...] * LOG2E

    def chunk(ci, carry):
        r0 = pl.multiple_of(ci * C, C)
        la = _log2_sigmoid2(xa_ref[0, pl.ds(r0, C), :] + ba2) * (1.0 / GLA_TAU)
        la_hi = la.astype(BF16)
        rem = la - la_hi.astype(F32)
        la_mid = rem.astype(BF16)
        la_lo = (rem - la_mid.astype(F32)).astype(BF16)
        la3 = jnp.concatenate([la_hi, la_mid, la_lo], axis=0)
        ball = jnp.dot(cum_ref[...], la3, preferred_element_type=F32)
        q = q_ref[0, pl.ds(r0, C), :]
        k = k_ref[0, pl.ds(r0, C), :]
        v = v_ref[0, pl.ds(r0, C), :]
        gg = gg_ref[0, pl.ds(r0, C), :].astype(F32)
        for h in range(GLA_HEADS):
            ks = slice(h * GLA_DK, (h + 1) * GLA_DK)
            vs = slice(h * GLA_DV, (h + 1) * GLA_DV)
            b = ball[0:C, ks]
            qh, kh, vh = q[:, ks], k[:, ks], v[:, vs]
            st = state_ref[h]
            o = lax.dot_general((qh * jnp.exp2(b)).astype(BF16), st.astype(BF16),
                                (((1,), (1,)), ((), ())),
                                preferred_element_type=F32)
            sc = lax.dot_general(qh.astype(BF16), kh.astype(BF16), (((1,), (1,)), ((), ())),
                                 preferred_element_type=F32)
            p = jnp.where(pm_ref[0] != 0.0, sc, 0.0)
            for li in range(len(GLA_LEVELS)):
                anc = ball[(li + 1) * C:(li + 2) * C, ks]
                f = jnp.exp2(-jnp.abs(b - anc))
                sc = lax.dot_general((qh * f).astype(BF16), (kh * f).astype(BF16),
                                     (((1,), (1,)), ((), ())), preferred_element_type=F32)
                p = jnp.where(pm_ref[li + 1] != 0.0, sc, p)
            o = o + jnp.dot(p.astype(BF16), vh, preferred_element_type=F32)
            b_last = b[C - 1:C, :]
            kd = (kh * jnp.exp2(b_last - b)).astype(BF16)
            upd = lax.dot_general(vh, kd, (((0,), (0,)), ((), ())),
                                  preferred_element_type=F32)
            state_ref[h] = st * jnp.exp2(b_last) + upd
            on = o * lax.rsqrt(jnp.mean(o * o, axis=-1, keepdims=True) + LN_EPS)
            g = gg[:, vs]
            on = on * gn_ref[:, vs] * (g * _sigmoid(g))
            o_ref[0, pl.ds(r0, C), vs] = on.astype(BF16)
        return carry

    lax.fori_loop(0, ts // C, chunk, 0, unroll=2)


def _gla(hf3, hb3, b_alpha, gla_norm_g):
    B, S, _ = hf3.shape
    ts = 512
    kern = functools.partial(_gla_kernel, ts=ts)
    n_masks = 1 + len(GLA_LEVELS)
    return pl.pallas_call(
        kern,
        grid=(B, S // ts),
        in_specs=[pl.BlockSpec((1, ts, GLA_QK), lambda b, s: (b, s, 0)),
                  pl.BlockSpec((1, ts, GLA_QK), lambda b, s: (b, s, 1)),
                  pl.BlockSpec((1, ts, GLA_QK), lambda b, s: (b, s, 2)),
                  pl.BlockSpec((1, ts, GLA_VW), lambda b, s: (b, s, HB_GV)),
                  pl.BlockSpec((1, ts, GLA_VW), lambda b, s: (b, s, HB_GG)),
                  pl.BlockSpec((1, GLA_QK), lambda b, s: (0, 0)),
                  pl.BlockSpec((1, GLA_VW), lambda b, s: (0, 0)),
                  pl.BlockSpec((n_masks, GLA_CHUNK, GLA_CHUNK), lambda b, s: (0, 0, 0)),
                  pl.BlockSpec((n_masks * GLA_CHUNK, 3 * GLA_CHUNK), lambda b, s: (0, 0))],
        out_specs=pl.BlockSpec((1, ts, GLA_VW), lambda b, s: (b, s, 0)),
        out_shape=jax.ShapeDtypeStruct((B, S, GLA_VW), BF16),
        scratch_shapes=[pltpu.VMEM((GLA_HEADS, GLA_DV, GLA_DK), F32)],
        compiler_params=pltpu.CompilerParams(
            dimension_semantics=("arbitrary", "arbitrary"), vmem_limit_bytes=VMEM_LIMIT),
        name="gla",
    )(hf3, hf3, hf3, hb3, hb3, b_alpha.reshape(1, GLA_QK), gla_norm_g.reshape(1, GLA_VW),
      jnp.asarray(_gla_pair_masks()), jnp.asarray(_gla_cum_matrix(), dtype=BF16))


SB_TQ = 128
SB_SUB = 128
SB_NSUB = 3
SB_HP = 4
SB_MASKED = -1e30
SB_DEAD2 = -152.0


def _sb_suffix_matrix():
    j = np.arange(2 * SB_SUB)[:, None] % SB_SUB
    s = np.arange(2 * SB_SUB)[None, :]
    return ((s >= SB_SUB) | (j > s)).astype(np.float32)


def _sb_causal_bias():
    r = np.arange(SB_TQ)[:, None]
    c = np.arange(SB_TQ)[None, :]
    return np.where(c < r, 0.0, SB_MASKED).astype(np.float32)


def _sb_scores(q, k):
    return lax.dot_general(q, k, (((1,), (1,)), ((), ())), preferred_element_type=F32)


def _sb_split(z):
    n_sub = z.shape[1] // SB_SUB
    ls = _log2_sigmoid2(z)
    lk = ls - z
    hi = lk.astype(BF16)
    lo = (lk - hi.astype(F32)).astype(BF16)
    parts = []
    for c in range(n_sub):
        cs = slice(c * SB_SUB, (c + 1) * SB_SUB)
        parts += [hi[:, cs], lo[:, cs]]
    return ls, jnp.concatenate(parts, axis=1)


def _sb_weights(ls, hl, carry, mo):
    n_sub = ls.shape[1] // SB_SUB
    ws = [None] * n_sub
    for c in reversed(range(n_sub)):
        r = jnp.dot(hl[:, 2 * c * SB_SUB:2 * (c + 1) * SB_SUB], mo, preferred_element_type=F32)
        ws[c] = jnp.exp2(ls[:, c * SB_SUB:(c + 1) * SB_SUB] + (r[:, :SB_SUB] + carry)).astype(BF16)
        carry = carry + r[:, SB_SUB:]
    return (jnp.concatenate(ws, axis=1) if n_sub > 1 else ws[0]), carry


def _sb_kernel(q_ref, k_ref, v_ref, mo_ref, bias_ref, o_ref, z_s, ls_s, hl_s, w_s, c_s, live_s):
    T = SB_TQ
    W = SB_NSUB * SB_SUB
    n_q = q_ref.shape[1] // T
    heads = [slice(h * SB_DH, (h + 1) * SB_DH) for h in range(SB_HP)]
    zero = jnp.zeros((T, SB_SUB), F32)

    def row0(i):
        return i * T if isinstance(i, int) else pl.multiple_of(i * T, T)

    def whole_history(i):
        q0 = row0(i)
        qs = [q_ref[0, pl.ds(q0, T), hs] for hs in heads]
        carries, accs = [], []
        for h, hs in enumerate(heads):
            ls, hl = _sb_split(_sb_scores(qs[h], k_ref[0, pl.ds(q0, T), hs]) + bias_ref[...])
            w, c = _sb_weights(ls, hl, zero, mo_ref[...])
            carries.append(c)
            accs.append(jnp.dot(w, v_ref[0, pl.ds(q0, T), hs], preferred_element_type=F32))

        def cond(st):
            kk, carries, _ = st
            live = functools.reduce(jnp.maximum, carries)
            return jnp.logical_and(kk >= 0, jnp.max(live) > SB_DEAD2)

        def body(st):
            kk, carries, accs = st
            kk = pl.multiple_of(kk, SB_SUB)
            new_c, new_a = [], []
            for h, hs in enumerate(heads):
                ls, hl = _sb_split(_sb_scores(qs[h], k_ref[0, pl.ds(kk, SB_SUB), hs]))
                w, c = _sb_weights(ls, hl, carries[h], mo_ref[...])
                new_c.append(c)
                new_a.append(accs[h] + jnp.dot(w, v_ref[0, pl.ds(kk, SB_SUB), hs],
                                               preferred_element_type=F32))
            return kk - SB_SUB, tuple(new_c), tuple(new_a)

        _, _, accs = lax.while_loop(cond, body, (q0 - SB_SUB, tuple(carries), tuple(accs)))
        for h, hs in enumerate(heads):
            o_ref[0, pl.ds(q0, T), hs] = accs[h].astype(BF16)

    def scores_stage(i, slot):
        q0 = row0(i)
        k0 = row0(i + 1 - SB_NSUB)
        for h, hs in enumerate(heads):
            z = _sb_scores(q_ref[0, pl.ds(q0, T), hs], k_ref[0, pl.ds(k0, W), hs])
            z_s[slot, h] = jnp.concatenate(
                [z[:, :W - SB_SUB], z[:, W - SB_SUB:] + bias_ref[...]], axis=1)

    def split_stage(slot):
        for h in range(SB_HP):
            ls, hl = _sb_split(z_s[slot, h])
            ls_s[slot, h] = ls
            hl_s[slot, h] = hl

    def weights_stage(slot):
        for h in range(SB_HP):
            w, c = _sb_weights(ls_s[slot, h], hl_s[slot, h], zero, mo_ref[...])
            w_s[slot, h] = w
            c_s[slot, h] = c

    def output_stage(i, slot):
        q0 = row0(i)
        k0 = row0(i + 1 - SB_NSUB)
        for h, hs in enumerate(heads):
            acc = jnp.dot(w_s[slot, h], v_ref[0, pl.ds(k0, W), hs], preferred_element_type=F32)
            o_ref[0, pl.ds(q0, T), hs] = acc.astype(BF16)
        live = functools.reduce(jnp.maximum, [c_s[slot, h] for h in range(SB_HP)])
        live_s[i] = jnp.max(live)

    first = SB_NSUB - 1

    def steady(j, s):
        scores_stage(j, s)
        split_stage(1 - s)
        weights_stage(s)
        output_stage(j - 3, 1 - s)

    scores_stage(first, 0)
    scores_stage(first + 1, 1)
    split_stage(0)
    scores_stage(first + 2, 0)
    split_stage(1)
    weights_stage(0)

    def pair(p, carry):
        j = first + 3 + 2 * p
        steady(j, 1)
        steady(j + 1, 0)
        return carry

    n_pairs = (n_q - first - 3) // 2
    lax.fori_loop(0, n_pairs, pair, 0)
    for j in range(first + 3 + 2 * n_pairs, n_q):
        steady(j, (j - first) % 2)
    last = (n_q - 1 - first) % 2
    split_stage(last)
    weights_stage(1 - last)
    output_stage(n_q - 3, last)
    weights_stage(last)
    output_stage(n_q - 2, 1 - last)
    output_stage(n_q - 1, last)

    for i in range(first):
        whole_history(i)

    def second_pass(i, carry):
        @pl.when(live_s[i] > SB_DEAD2)
        def _():
            whole_history(i)
        return carry

    lax.fori_loop(first, n_q, second_pass, 0)


def _stickbreak(hb3):
    B, S, _ = hb3.shape
    wblk = SB_HP * SB_DH
    per_group = D_MODEL // wblk
    blk = lambda group: pl.BlockSpec((1, S, wblk), lambda b, g: (b, 0, group * per_group + g))
    const = lambda a: pl.BlockSpec(a.shape, lambda b, g: (0, 0))
    mo = jnp.asarray(_sb_suffix_matrix(), dtype=BF16)
    bias = jnp.asarray(_sb_causal_bias())
    win = SB_NSUB * SB_SUB
    return pl.pallas_call(
        _sb_kernel,
        grid=(B, SB_HEADS // SB_HP),
        in_specs=[blk(HB_SQ), blk(HB_SK), blk(HB_SV), const(mo), const(bias)],
        out_specs=pl.BlockSpec((1, S, wblk), lambda b, g: (b, 0, g)),
        out_shape=jax.ShapeDtypeStruct((B, S, SB_HEADS * SB_DH), BF16),
        scratch_shapes=[pltpu.VMEM((2, SB_HP, SB_TQ, win), F32),
                        pltpu.VMEM((2, SB_HP, SB_TQ, win), F32),
                        pltpu.VMEM((2, SB_HP, SB_TQ, 2 * win), BF16),
                        pltpu.VMEM((2, SB_HP, SB_TQ, win), BF16),
                        pltpu.VMEM((2, SB_HP, SB_TQ, SB_SUB), F32),
                        pltpu.SMEM((S // SB_TQ,), F32)],
        compiler_params=pltpu.CompilerParams(
            dimension_semantics=("arbitrary", "arbitrary"), vmem_limit_bytes=VMEM_LIMIT),
        name="stickbreak",
    )(hb3, hb3, hb3, mo, bias)


def _merge_kernel(x_ref, og_ref, os_ref, ma_ref, mb_ref, mod_ref, w_ref, g_ref, b_ref, o_ref):
    merged = (_sigmoid(ma_ref[...].astype(F32)) * og_ref[...].astype(F32)
              + _sigmoid(mb_ref[...].astype(F32)) * os_ref[...].astype(F32))
    y = jnp.dot(merged.astype(BF16), w_ref[...], preferred_element_type=F32)
    gate = mod_ref[0, :, 2 * D_MODEL:3 * D_MODEL]
    o_ref[...] = _ln(ALPHA * x_ref[...] + (1.0 + gate) * y) * g_ref[...] + b_ref[...]


def _merge_out(x2d, og2, os2, hb2, mod3, w_out, ln_g, ln_b, seq):
    M, D = x2d.shape
    tm = 512
    per_b = seq // tm
    row = lambda i: (i, 0)
    return pl.pallas_call(
        _merge_kernel,
        grid=(M // tm,),
        in_specs=[pl.BlockSpec((tm, D), row),
                  pl.BlockSpec((tm, D), row),
                  pl.BlockSpec((tm, D), row),
                  pl.BlockSpec((tm, D), lambda i: (i, HB_MA)),
                  pl.BlockSpec((tm, D), lambda i: (i, HB_MB)),
                  pl.BlockSpec((1, 1, 6 * D_MODEL), lambda i: (i // per_b, 0, 0)),
                  pl.BlockSpec((D, D), lambda i: (0, 0)),
                  pl.BlockSpec((1, D), lambda i: (0, 0)),
                  pl.BlockSpec((1, D), lambda i: (0, 0))],
        out_specs=pl.BlockSpec((tm, D), row),
        out_shape=jax.ShapeDtypeStruct((M, D), F32),
        compiler_params=pltpu.CompilerParams(
            dimension_semantics=("arbitrary",), vmem_limit_bytes=VMEM_LIMIT),
        name="merge_out",
    )(x2d, og2, os2, hb2, hb2, mod3, w_out, ln_g.reshape(1, D), ln_b.reshape(1, D))


FFN_TF = 1408


def _ffn_kernel(x_ref, mod_ref, wg_ref, wu_ref, wo_ref, g_ref, b_ref, o_ref, u_ref, acc_ref):
    j = pl.program_id(1)

    @pl.when(j == 0)
    def _():
        shift = mod_ref[0, :, 3 * D_MODEL:4 * D_MODEL]
        scale = mod_ref[0, :, 4 * D_MODEL:5 * D_MODEL]
        u_ref[...] = (_ln(x_ref[...]) * (1.0 + scale) + shift).astype(BF16)

    u = u_ref[...]
    g = jnp.dot(u, wg_ref[...], preferred_element_type=F32)
    up = jnp.dot(u, wu_ref[...], preferred_element_type=F32)
    a = (g * _sigmoid(g) * up).astype(BF16)
    part = jnp.dot(a, wo_ref[...], preferred_element_type=F32)

    @pl.when(j == 0)
    def _():
        acc_ref[...] = part

    @pl.when(j == pl.num_programs(1) - 1)
    def _():
        f = acc_ref[...] + part
        gate = mod_ref[0, :, 5 * D_MODEL:6 * D_MODEL]
        o_ref[...] = _ln(ALPHA * x_ref[...] + (1.0 + gate) * f) * g_ref[...] + b_ref[...]


def _ffn(x2d, mod3, w_in, w_out, ln_g, ln_b, seq):
    M, D = x2d.shape
    tm = 512
    per_b = seq // tm
    nf = D_FF // FFN_TF
    assert nf == 2
    return pl.pallas_call(
        _ffn_kernel,
        grid=(M // tm, nf),
        in_specs=[pl.BlockSpec((tm, D), lambda i, j: (i, 0)),
                  pl.BlockSpec((1, 1, 6 * D_MODEL), lambda i, j: (i // per_b, 0, 0)),
                  pl.BlockSpec((D, FFN_TF), lambda i, j: (0, j)),
                  pl.BlockSpec((D, FFN_TF), lambda i, j: (0, nf + j)),
                  pl.BlockSpec((FFN_TF, D), lambda i, j: (j, 0)),
                  pl.BlockSpec((1, D), lambda i, j: (0, 0)),
                  pl.BlockSpec((1, D), lambda i, j: (0, 0))],
        out_specs=pl.BlockSpec((tm, D), lambda i, j: (i, 0)),
        out_shape=jax.ShapeDtypeStruct((M, D), F32),
        scratch_shapes=[pltpu.VMEM((tm, D), BF16), pltpu.VMEM((tm, D), F32)],
        compiler_params=pltpu.CompilerParams(
            dimension_semantics=("arbitrary", "arbitrary"), vmem_limit_bytes=VMEM_LIMIT),
        name="ffn",
    )(x2d, mod3, w_in, w_in, w_out, ln_g.reshape(1, D), ln_b.reshape(1, D))


def _regroup_w_in(w_in_l, w_decay):
    o = 0
    parts = {}
    for name, wdt in (("gq", GLA_QK), ("gk", GLA_QK), ("gv", GLA_VW), ("gg", GLA_VW),
                      ("ga", GLA_LOWRANK), ("sq", D_MODEL), ("sk", D_MODEL), ("sv", D_MODEL),
                      ("ma", D_MODEL), ("mb", D_MODEL)):
        parts[name] = w_in_l[:, o:o + wdt]
        o += wdt
    w_all = jnp.concatenate(
        [parts["gq"], parts["gk"], w_decay, parts["gv"], parts["gg"], parts["sq"],
         parts["sk"], parts["sv"], parts["ma"], parts["mb"]], axis=1).astype(BF16)
    return w_all, parts["ga"]


def _colscale():
    ones = lambda n: jnp.ones((n,), F32)
    return jnp.concatenate([
        ones(GLA_QK) * (GLA_DK ** -0.5), ones(GLA_QK), ones(GLA_QK) * LOG2E, ones(2 * GLA_VW),
        ones(D_MODEL) * (SB_DH ** -0.5 * LOG2E), ones(4 * D_MODEL)]).reshape(1, HF_W + HB_W)


def kernel(x, c, w_ada, b_ada, w_in, w_alpha_up, b_alpha, gla_norm_g, w_out,
           ln1_g, ln1_b, w_ffn_in, w_ffn_out, ln2_g, ln2_b):
    B, S, D = x.shape
    M = B * S
    x2d = x.reshape(M, D)
    for l in range(DEPTH):
        mod3 = _adaln_mod(c, w_ada[l], b_ada[l]).reshape(B, 1, 6 * D)
        ga_off = 2 * GLA_QK + 2 * GLA_VW
        w_decay = _decay_weight(w_in[l][:, ga_off:ga_off + GLA_LOWRANK], w_alpha_up[l])
        w_all, _ = _regroup_w_in(w_in[l], w_decay)
        hf, hb = _in_proj(x2d, mod3, w_all, _colscale(), S)
        hf3 = hf.reshape(B, S, HF_W)
        hb3 = hb.reshape(B, S, HB_W)
        og = _gla(hf3, hb3, b_alpha[l], gla_norm_g[l])
        osb = _stickbreak(hb3)
        x2d = _merge_out(x2d, og.reshape(M, D), osb.reshape(M, D), hb, mod3,
                         w_out[l].astype(BF16), ln1_g[l], ln1_b[l], S)
        x2d = _ffn(x2d, mod3, w_ffn_in[l].astype(BF16), w_ffn_out[l].astype(BF16),
                   ln2_g[l], ln2_b[l], S)
    return x2d.reshape(B, S, D)
```

```python
import functools
import math

import numpy as np
import jax
import jax.numpy as jnp
from jax import lax
from jax.experimental import pallas as pl
from jax.experimental.pallas import tpu as pltpu

F32 = jnp.float32
BF16 = jnp.bfloat16

D_MODEL = 1024
GLA_HEADS = 4
GLA_DK = 128
GLA_DV = 256
GLA_QK = GLA_HEADS * GLA_DK
GLA_VW = GLA_HEADS * GLA_DV
GLA_LOWRANK = 16
GLA_TAU = 16.0
GLA_CHUNK = 64
SB_HEADS = 8
SB_DH = 128
D_FF = 2816
DEPTH = 1
ALPHA = (2.0 * DEPTH) ** 0.25
LN_EPS = 1e-5
LOG2E = math.log2(math.e)

HB_W = 8 * D_MODEL
IN_TN = 1024
HB_TILES = HB_W // IN_TN
HB_GQK, HB_GV, HB_GG, HB_SQ, HB_SK, HB_SV, HB_MA, HB_MB = 0, 1, 2, 3, 4, 5, 6, 7

VMEM_LIMIT = 56 * 1024 * 1024
LN_ROWS = 256


def _ln(x):
    mu = jnp.mean(x, axis=-1, keepdims=True)
    xc = x - mu
    var = jnp.mean(xc * xc, axis=-1, keepdims=True)
    return xc * lax.rsqrt(var + LN_EPS)


def _sigmoid(x):
    return 1.0 / (1.0 + jnp.exp(-x))


def _log2_sigmoid2(x2):
    return jnp.minimum(x2, 0.0) - jnp.log(1.0 + jnp.exp2(-jnp.abs(x2))) * LOG2E


def _adaln_kernel(c_ref, w_ref, b_ref, o_ref):
    c = c_ref[...]
    ca = c * _sigmoid(c)
    o_ref[...] = jnp.dot(ca, w_ref[...], preferred_element_type=F32,
                         precision=lax.Precision.HIGHEST) + b_ref[...]


def _adaln_mod(c, w_ada, b_ada):
    B, D = c.shape
    N = w_ada.shape[1]
    tn = 1024
    return pl.pallas_call(
        _adaln_kernel,
        grid=(N // tn,),
        in_specs=[pl.BlockSpec((B, D), lambda j: (0, 0)),
                  pl.BlockSpec((D, tn), lambda j: (0, j)),
                  pl.BlockSpec((1, tn), lambda j: (0, j))],
        out_specs=pl.BlockSpec((B, tn), lambda j: (0, j)),
        out_shape=jax.ShapeDtypeStruct((B, N), F32),
        name="adaln_mod",
    )(c, w_ada, b_ada.reshape(1, N))


def _decay_weight_kernel(wga_ref, wup_ref, o_ref):
    o_ref[...] = jnp.dot(wga_ref[...], wup_ref[...], preferred_element_type=F32,
                         precision=lax.Precision.HIGHEST)


def _decay_weight(w_ga, w_up):
    D = w_ga.shape[0]
    return pl.pallas_call(
        _decay_weight_kernel,
        out_shape=jax.ShapeDtypeStruct((D, GLA_QK), F32),
        name="decay_weight",
    )(w_ga, w_up)


def _inproj_kernel(x_ref, mod_ref, w_ref, cs_ref, wa_ref, hb_ref, xa_ref, u_ref):
    j = pl.program_id(1)

    @pl.when(j == 0)
    def _():
        shift = mod_ref[0, :, 0 * D_MODEL:1 * D_MODEL]
        scale1p = 1.0 + mod_ref[0, :, 1 * D_MODEL:2 * D_MODEL]

        def rows(t, carry):
            r0 = pl.multiple_of(t * LN_ROWS, LN_ROWS)
            u = _ln(x_ref[pl.ds(r0, LN_ROWS), :]) * scale1p + shift
            u_ref[pl.ds(r0, LN_ROWS), :] = u.astype(BF16)
            return carry

        lax.fori_loop(0, x_ref.shape[0] // LN_ROWS, rows, 0)
        xa_ref[...] = jnp.dot(u_ref[...], wa_ref[...], preferred_element_type=F32) * LOG2E

    acc = jnp.dot(u_ref[...], w_ref[...], preferred_element_type=F32) * cs_ref[...]
    hb_ref[...] = acc.astype(BF16)


def _in_proj(x2d, mod3, w_all, colscale, w_decay, seq):
    M, D = x2d.shape
    tm = 2048
    per_b = seq // tm
    return pl.pallas_call(
        _inproj_kernel,
        grid=(M // tm, HB_TILES),
        in_specs=[pl.BlockSpec((tm, D), lambda i, j: (i, 0)),
                  pl.BlockSpec((1, 1, 6 * D_MODEL), lambda i, j: (i // per_b, 0, 0)),
                  pl.BlockSpec((D, IN_TN), lambda i, j: (0, j)),
                  pl.BlockSpec((1, IN_TN), lambda i, j: (0, j)),
                  pl.BlockSpec((D, GLA_QK), lambda i, j: (0, 0))],
        out_specs=[pl.BlockSpec((tm, IN_TN), lambda i, j: (i, j)),
                   pl.BlockSpec((tm, GLA_QK), lambda i, j: (i, 0))],
        out_shape=[jax.ShapeDtypeStruct((M, HB_W), BF16),
                   jax.ShapeDtypeStruct((M, GLA_QK), F32)],
        scratch_shapes=[pltpu.VMEM((tm, D), BF16)],
        compiler_params=pltpu.CompilerParams(
            dimension_semantics=("arbitrary", "arbitrary"), vmem_limit_bytes=VMEM_LIMIT),
        name="in_proj",
    )(x2d, mod3, w_all, colscale, w_decay)


GLA_LEVELS = (32, 16, 8, 4, 2, 1)


def _gla_cum_matrix():
    C = GLA_CHUNK
    r = np.arange(C)[:, None]
    s = np.arange(C)[None, :]
    blocks = [s <= r]
    for blk in GLA_LEVELS:
        blocks.append(s <= (r // (2 * blk)) * (2 * blk) + blk - 1)
    one = np.concatenate(blocks, axis=0).astype(np.float32)
    return np.concatenate([one, one, one], axis=1)


def _gla_pair_masks():
    C = GLA_CHUNK
    r = np.arange(C)[:, None]
    s = np.arange(C)[None, :]
    masks = [r == s]
    for blk in GLA_LEVELS:
        masks.append(((r // blk) == (s // blk) + 1) & ((r // (2 * blk)) == (s // (2 * blk))))
    return np.stack(masks).astype(np.float32)


def _gla_kernel(q_ref, k_ref, xa_ref, v_ref, gg_ref, ba_ref, gn_ref, pm_ref, cum_ref, o_ref,
                state_ref, b_s, ql_s, kl_s, p_s, qe_s, kd_s, dec_s, *, ts):
    C = GLA_CHUNK
    n_levels = len(GLA_LEVELS)
    n_chunks = ts // C
    nt = (((1,), (1,)), ((), ()))
    kslices = [slice(h * GLA_DK, (h + 1) * GLA_DK) for h in range(GLA_HEADS)]
    vslices = [slice(h * GLA_DV, (h + 1) * GLA_DV) for h in range(GLA_HEADS)]

    @pl.when(pl.program_id(1) == 0)
    def _():
        state_ref[...] = jnp.zeros_like(state_ref)

    ba2 = ba_ref[...] * LOG2E

    def rows(n):
        return pl.ds(n * C if isinstance(n, int) else pl.multiple_of(n * C, C), C)

    def decay_stage(n, slot):
        la = _log2_sigmoid2(xa_ref[0, rows(n), :] + ba2) * (1.0 / GLA_TAU)
        la_hi = la.astype(BF16)
        rem = la - la_hi.astype(F32)
        la_mid = rem.astype(BF16)
        la_lo = (rem - la_mid.astype(F32)).astype(BF16)
        la3 = jnp.concatenate([la_hi, la_mid, la_lo], axis=0)
        ball = jnp.dot(cum_ref[...], la3, preferred_element_type=F32)
        q = q_ref[0, rows(n), :].astype(F32)
        k = k_ref[0, rows(n), :].astype(F32)
        for h, ks in enumerate(kslices):
            b = ball[0:C, ks]
            b_s[slot, h] = b
            for li in range(n_levels):
                f = jnp.exp2(-jnp.abs(b - ball[(li + 1) * C:(li + 2) * C, ks]))
                ql_s[slot, h, li] = (q[:, ks] * f).astype(BF16)
                kl_s[slot, h, li] = (k[:, ks] * f).astype(BF16)

    def score_stage(n, slot):
        q16 = q_ref[0, rows(n), :]
        k16 = k_ref[0, rows(n), :]
        q = q16.astype(F32)
        k = k16.astype(F32)
        for h, ks in enumerate(kslices):
            sc = lax.dot_general(q16[:, ks], k16[:, ks], nt, preferred_element_type=F32)
            p = jnp.where(pm_ref[0] != 0.0, sc, 0.0)
            for li in range(n_levels):
                sc = lax.dot_general(ql_s[slot, h, li], kl_s[slot, h, li], nt,
                                     preferred_element_type=F32)
                p = jnp.where(pm_ref[li + 1] != 0.0, sc, p)
            p_s[slot, h] = p.astype(BF16)
            b = b_s[slot, h]
            b_last = b[C - 1:C, :]
            qe_s[slot, h] = (q[:, ks] * jnp.exp2(b)).astype(BF16)
            kd_s[slot, h] = (k[:, ks] * jnp.exp2(b_last - b)).astype(BF16)
            dec_s[slot, h] = jnp.exp2(b_last)

    def output_stage(n, slot):
        v = v_ref[0, rows(n), :]
        gg = gg_ref[0, rows(n), :].astype(F32)
        for h, vs in enumerate(vslices):
            st = state_ref[h]
            vh = v[:, vs]
            o = (lax.dot_general(qe_s[slot, h], st.astype(BF16), nt, preferred_element_type=F32)
                 + jnp.dot(p_s[slot, h], vh, preferred_element_type=F32))
            upd = lax.dot_general(vh, kd_s[slot, h], (((0,), (0,)), ((), ())),
                                  preferred_element_type=F32)
            state_ref[h] = st * dec_s[slot, h] + upd
            on = o * lax.rsqrt(jnp.mean(o * o, axis=-1, keepdims=True) + LN_EPS)
            g = gg[:, vs]
            on = on * gn_ref[:, vs] * (g * _sigmoid(g))
            o_ref[0, rows(n), vs] = on.astype(BF16)

    decay_stage(0, 0)
    decay_stage(1, 1)
    score_stage(0, 0)

    def pair(p, carry):
        n = 2 * p
        decay_stage(n + 2, 0)
        score_stage(n + 1, 1)
        output_stage(n, 0)
        decay_stage(n + 3, 1)
        score_stage(n + 2, 0)
        output_stage(n + 1, 1)
        return carry

    assert n_chunks % 2 == 0
    lax.fori_loop(0, (n_chunks - 2) // 2, pair, 0)
    score_stage(n_chunks - 1, 1)
    output_stage(n_chunks - 2, 0)
    output_stage(n_chunks - 1, 1)


def _gla(hb3, xa3, b_alpha, gla_norm_g):
    B, S, _ = hb3.shape
    ts = 1024
    kern = functools.partial(_gla_kernel, ts=ts)
    n_masks = 1 + len(GLA_LEVELS)
    H, C, L = GLA_HEADS, GLA_CHUNK, len(GLA_LEVELS)
    return pl.pallas_call(
        kern,
        grid=(B, S // ts),
        in_specs=[pl.BlockSpec((1, ts, GLA_QK), lambda b, s: (b, s, 2 * HB_GQK)),
                  pl.BlockSpec((1, ts, GLA_QK), lambda b, s: (b, s, 2 * HB_GQK + 1)),
                  pl.BlockSpec((1, ts, GLA_QK), lambda b, s: (b, s, 0)),
                  pl.BlockSpec((1, ts, GLA_VW), lambda b, s: (b, s, HB_GV)),
                  pl.BlockSpec((1, ts, GLA_VW), lambda b, s: (b, s, HB_GG)),
                  pl.BlockSpec((1, GLA_QK), lambda b, s: (0, 0)),
                  pl.BlockSpec((1, GLA_VW), lambda b, s: (0, 0)),
                  pl.BlockSpec((n_masks, C, C), lambda b, s: (0, 0, 0)),
                  pl.BlockSpec((n_masks * C, 3 * C), lambda b, s: (0, 0))],
        out_specs=pl.BlockSpec((1, ts, GLA_VW), lambda b, s: (b, s, 0)),
        out_shape=jax.ShapeDtypeStruct((B, S, GLA_VW), BF16),
        scratch_shapes=[pltpu.VMEM((H, GLA_DV, GLA_DK), F32),
                        pltpu.VMEM((2, H, C, GLA_DK), F32),
                        pltpu.VMEM((2, H, L, C, GLA_DK), BF16),
                        pltpu.VMEM((2, H, L, C, GLA_DK), BF16),
                        pltpu.VMEM((2, H, C, C), BF16),
                        pltpu.VMEM((2, H, C, GLA_DK), BF16),
                        pltpu.VMEM((2, H, C, GLA_DK), BF16),
                        pltpu.VMEM((2, H, 1, GLA_DK), F32)],
        compiler_params=pltpu.CompilerParams(
            dimension_semantics=("arbitrary", "arbitrary"), vmem_limit_bytes=VMEM_LIMIT),
        name="gla",
    )(hb3, hb3, xa3, hb3, hb3, b_alpha.reshape(1, GLA_QK), gla_norm_g.reshape(1, GLA_VW),
      jnp.asarray(_gla_pair_masks()), jnp.asarray(_gla_cum_matrix(), dtype=BF16))


SB_TQ = 128
SB_SUB = 128
SB_NSUB = 3
SB_HP = 4
SB_MASKED = -1e30
SB_DEAD2 = -152.0


def _sb_suffix_matrix():
    j = np.arange(2 * SB_SUB)[:, None] % SB_SUB
    s = np.arange(2 * SB_SUB)[None, :]
    return ((s >= SB_SUB) | (j > s)).astype(np.float32)


def _sb_window_bias():
    r = np.arange(SB_TQ)[:, None]
    c = np.arange(SB_NSUB * SB_SUB)[None, :]
    return np.stack([np.where(c < r + d * SB_SUB, 0.0, SB_MASKED)
                     for d in range(SB_NSUB)]).astype(np.float32)


def _sb_scores(q, k):
    return lax.dot_general(q, k, (((1,), (1,)), ((), ())), preferred_element_type=F32)


def _sb_split(z):
    n_sub = z.shape[1] // SB_SUB
    ls = _log2_sigmoid2(z)
    lk = ls - z
    hi = lk.astype(BF16)
    lo = (lk - hi.astype(F32)).astype(BF16)
    parts = []
    for c in range(n_sub):
        cs = slice(c * SB_SUB, (c + 1) * SB_SUB)
        parts += [hi[:, cs], lo[:, cs]]
    return ls, jnp.concatenate(parts, axis=1)


def _sb_weights(ls, hl, carry, mo):
    n_sub = ls.shape[1] // SB_SUB
    ws = [None] * n_sub
    for c in reversed(range(n_sub)):
        r = jnp.dot(hl[:, 2 * c * SB_SUB:2 * (c + 1) * SB_SUB], mo, preferred_element_type=F32)
        ws[c] = jnp.exp2(ls[:, c * SB_SUB:(c + 1) * SB_SUB] + (r[:, :SB_SUB] + carry)).astype(BF16)
        carry = carry + r[:, SB_SUB:]
    return (jnp.concatenate(ws, axis=1) if n_sub > 1 else ws[0]), carry


def _sb_kernel(q_ref, k_ref, v_ref, mo_ref, bias_ref, o_ref, z_s, ls_s, hl_s, w_s, c_s, call_s,
               live_s):
    T = SB_TQ
    W = SB_NSUB * SB_SUB
    n_q = q_ref.shape[1] // T
    heads = [slice(h * SB_DH, (h + 1) * SB_DH) for h in range(SB_HP)]
    zero = jnp.zeros((T, SB_SUB), F32)

    def row0(i):
        return i * T if isinstance(i, int) else pl.multiple_of(i * T, T)

    def window(i):
        if isinstance(i, int):
            return max(i + 1 - SB_NSUB, 0) * T, min(i, SB_NSUB - 1)
        return row0(jnp.maximum(i + 1 - SB_NSUB, 0)), jnp.minimum(i, SB_NSUB - 1)

    def scores_stage(i, slot):
        q0 = row0(i)
        k0, d = window(i)
        for h, hs in enumerate(heads):
            z = _sb_scores(q_ref[0, pl.ds(q0, T), hs], k_ref[0, pl.ds(k0, W), hs])
            z_s[slot, h] = z + bias_ref[d]

    def split_stage(slot):
        for h in range(SB_HP):
            ls, hl = _sb_split(z_s[slot, h])
            ls_s[slot, h] = ls
            hl_s[slot, h] = hl

    def weights_stage(i, slot):
        for h in range(SB_HP):
            w, c = _sb_weights(ls_s[slot, h], hl_s[slot, h], zero, mo_ref[...])
            w_s[slot, h] = w
            c_s[slot, h] = c
            call_s[i, h] = c

    def output_stage(i, slot):
        q0 = row0(i)
        k0, _ = window(i)
        for h, hs in enumerate(heads):
            acc = jnp.dot(w_s[slot, h], v_ref[0, pl.ds(k0, W), hs], preferred_element_type=F32)
            o_ref[0, pl.ds(q0, T), hs] = acc.astype(BF16)
        live = functools.reduce(jnp.maximum, [c_s[slot, h] for h in range(SB_HP)])
        live_s[i] = jnp.max(live)

    def steady(j, s):
        scores_stage(j, s)
        split_stage(1 - s)
        weights_stage(j - 2, s)
        output_stage(j - 3, 1 - s)

    scores_stage(0, 0)
    scores_stage(1, 1)
    split_stage(0)
    scores_stage(2, 0)
    split_stage(1)
    weights_stage(0, 0)

    def pair(p, carry):
        j = 3 + 2 * p
        steady(j, 1)
        steady(j + 1, 0)
        return carry

    n_pairs = (n_q - 3) // 2
    lax.fori_loop(0, n_pairs, pair, 0)
    for j in range(3 + 2 * n_pairs, n_q):
        steady(j, j % 2)
    last = (n_q - 1) % 2
    split_stage(last)
    weights_stage(n_q - 2, 1 - last)
    output_stage(n_q - 3, last)
    weights_stage(n_q - 1, last)
    output_stage(n_q - 2, 1 - last)
    output_stage(n_q - 1, last)

    def continue_tile(i):
        q0 = row0(i)
        k0, _ = window(i)
        qs = [q_ref[0, pl.ds(q0, T), hs] for hs in heads]

        def cond(st):
            kk, carries, _ = st
            live = functools.reduce(jnp.maximum, carries)
            return jnp.logical_and(kk >= 0, jnp.max(live) > SB_DEAD2)

        def body(st):
            kk, carries, accs = st
            kk = pl.multiple_of(kk, SB_SUB)
            new_c, new_a = [], []
            for h, hs in enumerate(heads):
                ls, hl = _sb_split(_sb_scores(qs[h], k_ref[0, pl.ds(kk, SB_SUB), hs]))
                w, c = _sb_weights(ls, hl, carries[h], mo_ref[...])
                new_c.append(c)
                new_a.append(accs[h] + jnp.dot(w, v_ref[0, pl.ds(kk, SB_SUB), hs],
                                               preferred_element_type=F32))
            return kk - SB_SUB, tuple(new_c), tuple(new_a)

        carries = tuple(call_s[i, h] for h in range(SB_HP))
        accs = tuple(o_ref[0, pl.ds(q0, T), hs].astype(F32) for hs in heads)
        _, _, accs = lax.while_loop(cond, body, (k0 - SB_SUB, carries, accs))
        for h, hs in enumerate(heads):
            o_ref[0, pl.ds(q0, T), hs] = accs[h].astype(BF16)

    def second_pass(i, carry):
        @pl.when(live_s[i] > SB_DEAD2)
        def _():
            continue_tile(i)
        return carry

    lax.fori_loop(SB_NSUB, n_q, second_pass, 0)


def _stickbreak(hb3):
    B, S, _ = hb3.shape
    wblk = SB_HP * SB_DH
    per_group = D_MODEL // wblk
    blk = lambda group: pl.BlockSpec((1, S, wblk), lambda b, g: (b, 0, group * per_group + g))
    const = lambda a: pl.BlockSpec(a.shape, lambda b, g: (0, 0))
    mo = jnp.asarray(_sb_suffix_matrix(), dtype=BF16)
    bias = jnp.asarray(_sb_window_bias())
    win = SB_NSUB * SB_SUB
    return pl.pallas_call(
        _sb_kernel,
        grid=(B, SB_HEADS // SB_HP),
        in_specs=[blk(HB_SQ), blk(HB_SK), blk(HB_SV), const(mo),
                  pl.BlockSpec(bias.shape, lambda b, g: (0, 0, 0))],
        out_specs=pl.BlockSpec((1, S, wblk), lambda b, g: (b, 0, g)),
        out_shape=jax.ShapeDtypeStruct((B, S, SB_HEADS * SB_DH), BF16),
        scratch_shapes=[pltpu.VMEM((2, SB_HP, SB_TQ, win), F32),
                        pltpu.VMEM((2, SB_HP, SB_TQ, win), F32),
                        pltpu.VMEM((2, SB_HP, SB_TQ, 2 * win), BF16),
                        pltpu.VMEM((2, SB_HP, SB_TQ, win), BF16),
                        pltpu.VMEM((2, SB_HP, SB_TQ, SB_SUB), F32),
                        pltpu.VMEM((S // SB_TQ, SB_HP, SB_TQ, SB_SUB), F32),
                        pltpu.SMEM((S // SB_TQ,), F32)],
        compiler_params=pltpu.CompilerParams(
            dimension_semantics=("arbitrary", "arbitrary"), vmem_limit_bytes=VMEM_LIMIT),
        name="stickbreak",
    )(hb3, hb3, hb3, mo, bias)


def _merge_kernel(x_ref, og_ref, os_ref, ma_ref, mb_ref, mod_ref, w_ref, g_ref, b_ref, o_ref):
    merged = (_sigmoid(ma_ref[...].astype(F32)) * og_ref[...].astype(F32)
              + _sigmoid(mb_ref[...].astype(F32)) * os_ref[...].astype(F32))
    y = jnp.dot(merged.astype(BF16), w_ref[...], preferred_element_type=F32)
    gate = mod_ref[0, :, 2 * D_MODEL:3 * D_MODEL]
    o_ref[...] = _ln(ALPHA * x_ref[...] + (1.0 + gate) * y) * g_ref[...] + b_ref[...]


def _merge_out(x2d, og2, os2, hb2, mod3, w_out, ln_g, ln_b, seq):
    M, D = x2d.shape
    tm = 512
    per_b = seq // tm
    row = lambda i: (i, 0)
    return pl.pallas_call(
        _merge_kernel,
        grid=(M // tm,),
        in_specs=[pl.BlockSpec((tm, D), row),
                  pl.BlockSpec((tm, D), row),
                  pl.BlockSpec((tm, D), row),
                  pl.BlockSpec((tm, D), lambda i: (i, HB_MA)),
                  pl.BlockSpec((tm, D), lambda i: (i, HB_MB)),
                  pl.BlockSpec((1, 1, 6 * D_MODEL), lambda i: (i // per_b, 0, 0)),
                  pl.BlockSpec((D, D), lambda i: (0, 0)),
                  pl.BlockSpec((1, D), lambda i: (0, 0)),
                  pl.BlockSpec((1, D), lambda i: (0, 0))],
        out_specs=pl.BlockSpec((tm, D), row),
        out_shape=jax.ShapeDtypeStruct((M, D), F32),
        compiler_params=pltpu.CompilerParams(
            dimension_semantics=("arbitrary",), vmem_limit_bytes=VMEM_LIMIT),
        name="merge_out",
    )(x2d, og2, os2, hb2, hb2, mod3, w_out, ln_g.reshape(1, D), ln_b.reshape(1, D))


FFN_TF = 1408


def _ffn_kernel(x_ref, mod_ref, wg_ref, wu_ref, wo_ref, g_ref, b_ref, o_ref, u_ref, acc_ref):
    j = pl.program_id(1)

    @pl.when(j == 0)
    def _():
        shift = mod_ref[0, :, 3 * D_MODEL:4 * D_MODEL]
        scale = mod_ref[0, :, 4 * D_MODEL:5 * D_MODEL]
        u_ref[...] = (_ln(x_ref[...]) * (1.0 + scale) + shift).astype(BF16)

    u = u_ref[...]
    g = jnp.dot(u, wg_ref[...], preferred_element_type=F32)
    up = jnp.dot(u, wu_ref[...], preferred_element_type=F32)
    a = (g * _sigmoid(g) * up).astype(BF16)
    part = jnp.dot(a, wo_ref[...], preferred_element_type=F32)

    @pl.when(j == 0)
    def _():
        acc_ref[...] = part

    @pl.when(j == pl.num_programs(1) - 1)
    def _():
        f = acc_ref[...] + part
        gate = mod_ref[0, :, 5 * D_MODEL:6 * D_MODEL]
        o_ref[...] = _ln(ALPHA * x_ref[...] + (1.0 + gate) * f) * g_ref[...] + b_ref[...]


def _ffn(x2d, mod3, w_in, w_out, ln_g, ln_b, seq):
    M, D = x2d.shape
    tm = 512
    per_b = seq // tm
    nf = D_FF // FFN_TF
    assert nf == 2
    return pl.pallas_call(
        _ffn_kernel,
        grid=(M // tm, nf),
        in_specs=[pl.BlockSpec((tm, D), lambda i, j: (i, 0)),
                  pl.BlockSpec((1, 1, 6 * D_MODEL), lambda i, j: (i // per_b, 0, 0)),
                  pl.BlockSpec((D, FFN_TF), lambda i, j: (0, j)),
                  pl.BlockSpec((D, FFN_TF), lambda i, j: (0, nf + j)),
                  pl.BlockSpec((FFN_TF, D), lambda i, j: (j, 0)),
                  pl.BlockSpec((1, D), lambda i, j: (0, 0)),
                  pl.BlockSpec((1, D), lambda i, j: (0, 0))],
        out_specs=pl.BlockSpec((tm, D), lambda i, j: (i, 0)),
        out_shape=jax.ShapeDtypeStruct((M, D), F32),
        scratch_shapes=[pltpu.VMEM((tm, D), BF16), pltpu.VMEM((tm, D), F32)],
        compiler_params=pltpu.CompilerParams(
            dimension_semantics=("arbitrary", "arbitrary"), vmem_limit_bytes=VMEM_LIMIT),
        name="ffn",
    )(x2d, mod3, w_in, w_in, w_out, ln_g.reshape(1, D), ln_b.reshape(1, D))


def _regroup_w_in(w_in_l):
    o = 0
    parts = {}
    for name, wdt in (("gq", GLA_QK), ("gk", GLA_QK), ("gv", GLA_VW), ("gg", GLA_VW),
                      ("ga", GLA_LOWRANK), ("sq", D_MODEL), ("sk", D_MODEL), ("sv", D_MODEL),
                      ("ma", D_MODEL), ("mb", D_MODEL)):
        parts[name] = w_in_l[:, o:o + wdt]
        o += wdt
    w_all = jnp.concatenate(
        [parts["gq"], parts["gk"], parts["gv"], parts["gg"], parts["sq"],
         parts["sk"], parts["sv"], parts["ma"], parts["mb"]], axis=1).astype(BF16)
    return w_all, parts["ga"]


def _colscale():
    ones = lambda n: jnp.ones((n,), F32)
    return jnp.concatenate([
        ones(GLA_QK) * (GLA_DK ** -0.5), ones(GLA_QK), ones(2 * GLA_VW),
        ones(D_MODEL) * (SB_DH ** -0.5 * LOG2E), ones(4 * D_MODEL)]).reshape(1, HB_W)


def kernel(x, c, w_ada, b_ada, w_in, w_alpha_up, b_alpha, gla_norm_g, w_out,
           ln1_g, ln1_b, w_ffn_in, w_ffn_out, ln2_g, ln2_b):
    B, S, D = x.shape
    M = B * S
    x2d = x.reshape(M, D)
    for l in range(DEPTH):
        mod3 = _adaln_mod(c, w_ada[l], b_ada[l]).reshape(B, 1, 6 * D)
        w_all, w_ga = _regroup_w_in(w_in[l])
        w_decay = _decay_weight(w_ga, w_alpha_up[l]).astype(BF16)
        hb, xa = _in_proj(x2d, mod3, w_all, _colscale(), w_decay, S)
        hb3 = hb.reshape(B, S, HB_W)
        og = _gla(hb3, xa.reshape(B, S, GLA_QK), b_alpha[l], gla_norm_g[l])
        osb = _stickbreak(hb3)
        x2d = _merge_out(x2d, og.reshape(M, D), osb.reshape(M, D), hb, mod3,
                         w_out[l].astype(BF16), ln1_g[l], ln1_b[l], S)
        x2d = _ffn(x2d, mod3, w_ffn_in[l].astype(BF16), w_ffn_out[l].astype(BF16),
                   ln2_g[l], ln2_b[l], S)
    return x2d.reshape(B, S, D)
```

```python
import functools
import math

import numpy as np
import jax
import jax.numpy as jnp
from jax import lax
from jax.experimental import pallas as pl
from jax.experimental.pallas import tpu as pltpu

F32 = jnp.float32
BF16 = jnp.bfloat16

D_MODEL = 1024
GLA_HEADS = 4
GLA_DK = 128
GLA_DV = 256
GLA_QK = GLA_HEADS * GLA_DK
GLA_VW = GLA_HEADS * GLA_DV
GLA_LOWRANK = 16
GLA_TAU = 16.0
GLA_CHUNK = 64
SB_HEADS = 8
SB_DH = 128
D_FF = 2816
DEPTH = 1
ALPHA = (2.0 * DEPTH) ** 0.25
LN_EPS = 1e-5
LOG2E = math.log2(math.e)

HB_W = 8 * D_MODEL
IN_TN = 1024
HB_TILES = HB_W // IN_TN
HB_GQK, HB_GV, HB_GG, HB_SQ, HB_SK, HB_SV, HB_MA, HB_MB = 0, 1, 2, 3, 4, 5, 6, 7

VMEM_LIMIT = 56 * 1024 * 1024
LN_ROWS = 256


def _ln(x):
    mu = jnp.mean(x, axis=-1, keepdims=True)
    xc = x - mu
    var = jnp.mean(xc * xc, axis=-1, keepdims=True)
    return xc * lax.rsqrt(var + LN_EPS)


def _sigmoid(x):
    return 1.0 / (1.0 + jnp.exp(-x))


def _log2_sigmoid2(x2):
    return jnp.minimum(x2, 0.0) - jnp.log(1.0 + jnp.exp2(-jnp.abs(x2))) * LOG2E


def _adaln_kernel(c_ref, w_ref, b_ref, o_ref):
    c = c_ref[...]
    ca = c * _sigmoid(c)
    o_ref[...] = jnp.dot(ca, w_ref[...], preferred_element_type=F32,
                         precision=lax.Precision.HIGHEST) + b_ref[...]


def _adaln_mod(c, w_ada, b_ada):
    B, D = c.shape
    N = w_ada.shape[1]
    tn = 1024
    return pl.pallas_call(
        _adaln_kernel,
        grid=(N // tn,),
        in_specs=[pl.BlockSpec((B, D), lambda j: (0, 0)),
                  pl.BlockSpec((D, tn), lambda j: (0, j)),
                  pl.BlockSpec((1, tn), lambda j: (0, j))],
        out_specs=pl.BlockSpec((B, tn), lambda j: (0, j)),
        out_shape=jax.ShapeDtypeStruct((B, N), F32),
        name="adaln_mod",
    )(c, w_ada, b_ada.reshape(1, N))


def _decay_weight_kernel(wga_ref, wup_ref, o_ref):
    o_ref[...] = jnp.dot(wga_ref[...], wup_ref[...], preferred_element_type=F32,
                         precision=lax.Precision.HIGHEST)


def _decay_weight(w_ga, w_up):
    D = w_ga.shape[0]
    return pl.pallas_call(
        _decay_weight_kernel,
        out_shape=jax.ShapeDtypeStruct((D, GLA_QK), F32),
        name="decay_weight",
    )(w_ga, w_up)


def _inproj_kernel(x_ref, mod_ref, w_ref, cs_ref, wa_ref, hb_ref, xa_ref, u_ref):
    j = pl.program_id(1)

    @pl.when(j == 0)
    def _():
        shift = mod_ref[0, :, 0 * D_MODEL:1 * D_MODEL]
        scale1p = 1.0 + mod_ref[0, :, 1 * D_MODEL:2 * D_MODEL]

        def rows(t, carry):
            r0 = pl.multiple_of(t * LN_ROWS, LN_ROWS)
            u = _ln(x_ref[pl.ds(r0, LN_ROWS), :]) * scale1p + shift
            u_ref[pl.ds(r0, LN_ROWS), :] = u.astype(BF16)
            return carry

        lax.fori_loop(0, x_ref.shape[0] // LN_ROWS, rows, 0)
        xa_ref[...] = jnp.dot(u_ref[...], wa_ref[...], preferred_element_type=F32) * LOG2E

    acc = jnp.dot(u_ref[...], w_ref[...], preferred_element_type=F32) * cs_ref[...]
    hb_ref[...] = acc.astype(BF16)


def _in_proj(x2d, mod3, w_all, colscale, w_decay, seq):
    M, D = x2d.shape
    tm = 2048
    per_b = seq // tm
    return pl.pallas_call(
        _inproj_kernel,
        grid=(M // tm, HB_TILES),
        in_specs=[pl.BlockSpec((tm, D), lambda i, j: (i, 0)),
                  pl.BlockSpec((1, 1, 6 * D_MODEL), lambda i, j: (i // per_b, 0, 0)),
                  pl.BlockSpec((D, IN_TN), lambda i, j: (0, j)),
                  pl.BlockSpec((1, IN_TN), lambda i, j: (0, j)),
                  pl.BlockSpec((D, GLA_QK), lambda i, j: (0, 0))],
        out_specs=[pl.BlockSpec((tm, IN_TN), lambda i, j: (i, j)),
                   pl.BlockSpec((tm, GLA_QK), lambda i, j: (i, 0))],
        out_shape=[jax.ShapeDtypeStruct((M, HB_W), BF16),
                   jax.ShapeDtypeStruct((M, GLA_QK), F32)],
        scratch_shapes=[pltpu.VMEM((tm, D), BF16)],
        compiler_params=pltpu.CompilerParams(
            dimension_semantics=("arbitrary", "arbitrary"), vmem_limit_bytes=VMEM_LIMIT),
        name="in_proj",
    )(x2d, mod3, w_all, colscale, w_decay)


GLA_LEVELS = (32, 16, 8, 4, 2, 1)


def _gla_cum_matrix():
    C = GLA_CHUNK
    r = np.arange(C)[:, None]
    s = np.arange(C)[None, :]
    blocks = [s <= r]
    for blk in GLA_LEVELS:
        blocks.append(s <= (r // (2 * blk)) * (2 * blk) + blk - 1)
    one = np.concatenate(blocks, axis=0).astype(np.float32)
    return np.concatenate([one, one, one], axis=1)


def _gla_pair_masks():
    C = GLA_CHUNK
    r = np.arange(C)[:, None]
    s = np.arange(C)[None, :]
    masks = [r == s]
    for blk in GLA_LEVELS:
        masks.append(((r // blk) == (s // blk) + 1) & ((r // (2 * blk)) == (s // (2 * blk))))
    return np.stack(masks).astype(np.float32)


def _gla_kernel(q_ref, k_ref, xa_ref, v_ref, gg_ref, ba_ref, gn_ref, pm_ref, cum_ref, o_ref,
                state_ref, b_s, ql_s, kl_s, p_s, qe_s, kd_s, dec_s, *, ts):
    C = GLA_CHUNK
    n_levels = len(GLA_LEVELS)
    n_chunks = ts // C
    nt = (((1,), (1,)), ((), ()))
    kslices = [slice(h * GLA_DK, (h + 1) * GLA_DK) for h in range(GLA_HEADS)]
    vslices = [slice(h * GLA_DV, (h + 1) * GLA_DV) for h in range(GLA_HEADS)]

    @pl.when(pl.program_id(1) == 0)
    def _():
        state_ref[...] = jnp.zeros_like(state_ref)

    ba2 = ba_ref[...] * LOG2E

    def rows(n):
        return pl.ds(n * C if isinstance(n, int) else pl.multiple_of(n * C, C), C)

    def decay_stage(n, slot):
        la = _log2_sigmoid2(xa_ref[0, rows(n), :] + ba2) * (1.0 / GLA_TAU)
        la_hi = la.astype(BF16)
        rem = la - la_hi.astype(F32)
        la_mid = rem.astype(BF16)
        la_lo = (rem - la_mid.astype(F32)).astype(BF16)
        la3 = jnp.concatenate([la_hi, la_mid, la_lo], axis=0)
        ball = jnp.dot(cum_ref[...], la3, preferred_element_type=F32)
        q = q_ref[0, rows(n), :].astype(F32)
        k = k_ref[0, rows(n), :].astype(F32)
        for h, ks in enumerate(kslices):
            b = ball[0:C, ks]
            b_s[slot, h] = b
            for li in range(n_levels):
                f = jnp.exp2(-jnp.abs(b - ball[(li + 1) * C:(li + 2) * C, ks]))
                ql_s[slot, h, li] = (q[:, ks] * f).astype(BF16)
                kl_s[slot, h, li] = (k[:, ks] * f).astype(BF16)

    def score_stage(n, slot):
        q16 = q_ref[0, rows(n), :]
        k16 = k_ref[0, rows(n), :]
        q = q16.astype(F32)
        k = k16.astype(F32)
        for h, ks in enumerate(kslices):
            sc = lax.dot_general(q16[:, ks], k16[:, ks], nt, preferred_element_type=F32)
            p = jnp.where(pm_ref[0] != 0.0, sc, 0.0)
            for li in range(n_levels):
                sc = lax.dot_general(ql_s[slot, h, li], kl_s[slot, h, li], nt,
                                     preferred_element_type=F32)
                p = jnp.where(pm_ref[li + 1] != 0.0, sc, p)
            p_s[slot, h] = p.astype(BF16)
            b = b_s[slot, h]
            b_last = b[C - 1:C, :]
            qe_s[slot, h] = (q[:, ks] * jnp.exp2(b)).astype(BF16)
            kd_s[slot, h] = (k[:, ks] * jnp.exp2(b_last - b)).astype(BF16)
            dec_s[slot, h] = jnp.exp2(b_last)

    def output_stage(n, slot):
        v = v_ref[0, rows(n), :]
        gg = gg_ref[0, rows(n), :].astype(F32)
        for h, vs in enumerate(vslices):
            st = state_ref[h]
            vh = v[:, vs]
            o = (lax.dot_general(qe_s[slot, h], st.astype(BF16), nt, preferred_element_type=F32)
                 + jnp.dot(p_s[slot, h], vh, preferred_element_type=F32))
            upd = lax.dot_general(vh, kd_s[slot, h], (((0,), (0,)), ((), ())),
                                  preferred_element_type=F32)
            state_ref[h] = st * dec_s[slot, h] + upd
            on = o * lax.rsqrt(jnp.mean(o * o, axis=-1, keepdims=True) + LN_EPS)
            g = gg[:, vs]
            on = on * gn_ref[:, vs] * (g * _sigmoid(g))
            o_ref[0, rows(n), vs] = on.astype(BF16)

    decay_stage(0, 0)
    decay_stage(1, 1)
    score_stage(0, 0)

    def pair(p, carry):
        n = 2 * p
        decay_stage(n + 2, 0)
        score_stage(n + 1, 1)
        output_stage(n, 0)
        decay_stage(n + 3, 1)
        score_stage(n + 2, 0)
        output_stage(n + 1, 1)
        return carry

    assert n_chunks % 2 == 0
    lax.fori_loop(0, (n_chunks - 2) // 2, pair, 0)
    score_stage(n_chunks - 1, 1)
    output_stage(n_chunks - 2, 0)
    output_stage(n_chunks - 1, 1)


def _gla(hb3, xa3, b_alpha, gla_norm_g):
    B, S, _ = hb3.shape
    ts = 1024
    kern = functools.partial(_gla_kernel, ts=ts)
    n_masks = 1 + len(GLA_LEVELS)
    H, C, L = GLA_HEADS, GLA_CHUNK, len(GLA_LEVELS)
    return pl.pallas_call(
        kern,
        grid=(B, S // ts),
        in_specs=[pl.BlockSpec((1, ts, GLA_QK), lambda b, s: (b, s, 2 * HB_GQK)),
                  pl.BlockSpec((1, ts, GLA_QK), lambda b, s: (b, s, 2 * HB_GQK + 1)),
                  pl.BlockSpec((1, ts, GLA_QK), lambda b, s: (b, s, 0)),
                  pl.BlockSpec((1, ts, GLA_VW), lambda b, s: (b, s, HB_GV)),
                  pl.BlockSpec((1, ts, GLA_VW), lambda b, s: (b, s, HB_GG)),
                  pl.BlockSpec((1, GLA_QK), lambda b, s: (0, 0)),
                  pl.BlockSpec((1, GLA_VW), lambda b, s: (0, 0)),
                  pl.BlockSpec((n_masks, C, C), lambda b, s: (0, 0, 0)),
                  pl.BlockSpec((n_masks * C, 3 * C), lambda b, s: (0, 0))],
        out_specs=pl.BlockSpec((1, ts, GLA_VW), lambda b, s: (b, s, 0)),
        out_shape=jax.ShapeDtypeStruct((B, S, GLA_VW), BF16),
        scratch_shapes=[pltpu.VMEM((H, GLA_DV, GLA_DK), F32),
                        pltpu.VMEM((2, H, C, GLA_DK), F32),
                        pltpu.VMEM((2, H, L, C, GLA_DK), BF16),
                        pltpu.VMEM((2, H, L, C, GLA_DK), BF16),
                        pltpu.VMEM((2, H, C, C), BF16),
                        pltpu.VMEM((2, H, C, GLA_DK), BF16),
                        pltpu.VMEM((2, H, C, GLA_DK), BF16),
                        pltpu.VMEM((2, H, 1, GLA_DK), F32)],
        compiler_params=pltpu.CompilerParams(
            dimension_semantics=("arbitrary", "arbitrary"), vmem_limit_bytes=VMEM_LIMIT),
        name="gla",
    )(hb3, hb3, xa3, hb3, hb3, b_alpha.reshape(1, GLA_QK), gla_norm_g.reshape(1, GLA_VW),
      jnp.asarray(_gla_pair_masks()), jnp.asarray(_gla_cum_matrix(), dtype=BF16))


SB_TQ = 128
SB_SUB = 128
SB_NSUB = 3
SB_HP = 4
SB_MASKED = -1e30
SB_DEAD2 = -152.0


def _sb_suffix_matrix():
    j = np.arange(2 * SB_SUB)[:, None] % SB_SUB
    s = np.arange(2 * SB_SUB)[None, :]
    return ((s >= SB_SUB) | (j > s)).astype(np.float32)


def _sb_window_bias():
    r = np.arange(SB_TQ)[:, None]
    c = np.arange(SB_NSUB * SB_SUB)[None, :]
    return np.stack([np.where(c < r + d * SB_SUB, 0.0, SB_MASKED)
                     for d in range(SB_NSUB)]).astype(np.float32)


def _sb_scores(q, k):
    return lax.dot_general(q, k, (((1,), (1,)), ((), ())), preferred_element_type=F32)


def _sb_split(z):
    n_sub = z.shape[1] // SB_SUB
    ls = _log2_sigmoid2(z)
    lk = ls - z
    hi = lk.astype(BF16)
    lo = (lk - hi.astype(F32)).astype(BF16)
    parts = []
    for c in range(n_sub):
        cs = slice(c * SB_SUB, (c + 1) * SB_SUB)
        parts += [hi[:, cs], lo[:, cs]]
    return ls, jnp.concatenate(parts, axis=1)


def _sb_weights(ls, hl, carry, mo):
    n_sub = ls.shape[1] // SB_SUB
    ws = [None] * n_sub
    for c in reversed(range(n_sub)):
        r = jnp.dot(hl[:, 2 * c * SB_SUB:2 * (c + 1) * SB_SUB], mo, preferred_element_type=F32)
        ws[c] = jnp.exp2(ls[:, c * SB_SUB:(c + 1) * SB_SUB] + (r[:, :SB_SUB] + carry)).astype(BF16)
        carry = carry + r[:, SB_SUB:]
    return (jnp.concatenate(ws, axis=1) if n_sub > 1 else ws[0]), carry


def _sb_kernel(q_ref, k_ref, v_ref, mo_ref, bias_ref, o_ref, z_s, ls_s, hl_s, w_s, c_s, call_s,
               live_s):
    T = SB_TQ
    W = SB_NSUB * SB_SUB
    n_q = q_ref.shape[1] // T
    heads = [slice(h * SB_DH, (h + 1) * SB_DH) for h in range(SB_HP)]
    zero = jnp.zeros((T, SB_SUB), F32)

    def row0(i):
        return i * T if isinstance(i, int) else pl.multiple_of(i * T, T)

    def window(i):
        if isinstance(i, int):
            return max(i + 1 - SB_NSUB, 0) * T, min(i, SB_NSUB - 1)
        return row0(jnp.maximum(i + 1 - SB_NSUB, 0)), jnp.minimum(i, SB_NSUB - 1)

    def scores_stage(i, slot):
        q0 = row0(i)
        k0, d = window(i)
        for h, hs in enumerate(heads):
            z = _sb_scores(q_ref[0, pl.ds(q0, T), hs], k_ref[0, pl.ds(k0, W), hs])
            z_s[slot, h] = z + bias_ref[d]

    def split_stage(slot):
        for h in range(SB_HP):
            ls, hl = _sb_split(z_s[slot, h])
            ls_s[slot, h] = ls
            hl_s[slot, h] = hl

    def weights_stage(i, slot):
        for h in range(SB_HP):
            w, c = _sb_weights(ls_s[slot, h], hl_s[slot, h], zero, mo_ref[...])
            w_s[slot, h] = w
            c_s[slot, h] = c
            call_s[i, h] = c

    def output_stage(i, slot):
        q0 = row0(i)
        k0, _ = window(i)
        for h, hs in enumerate(heads):
            acc = jnp.dot(w_s[slot, h], v_ref[0, pl.ds(k0, W), hs], preferred_element_type=F32)
            o_ref[0, pl.ds(q0, T), hs] = acc.astype(BF16)
        live = functools.reduce(jnp.maximum, [c_s[slot, h] for h in range(SB_HP)])
        live_s[i] = jnp.max(live)

    def steady(j, s):
        scores_stage(j, s)
        split_stage(1 - s)
        weights_stage(j - 2, s)
        output_stage(j - 3, 1 - s)

    scores_stage(0, 0)
    scores_stage(1, 1)
    split_stage(0)
    scores_stage(2, 0)
    split_stage(1)
    weights_stage(0, 0)

    def pair(p, carry):
        j = 3 + 2 * p
        steady(j, 1)
        steady(j + 1, 0)
        return carry

    n_pairs = (n_q - 3) // 2
    lax.fori_loop(0, n_pairs, pair, 0)
    for j in range(3 + 2 * n_pairs, n_q):
        steady(j, j % 2)
    last = (n_q - 1) % 2
    split_stage(last)
    weights_stage(n_q - 2, 1 - last)
    output_stage(n_q - 3, last)
    weights_stage(n_q - 1, last)
    output_stage(n_q - 2, 1 - last)
    output_stage(n_q - 1, last)

    def continue_tile(i):
        q0 = row0(i)
        k0, _ = window(i)
        qs = [q_ref[0, pl.ds(q0, T), hs] for hs in heads]

        def cond(st):
            kk, carries, _ = st
            live = functools.reduce(jnp.maximum, carries)
            return jnp.logical_and(kk >= 0, jnp.max(live) > SB_DEAD2)

        def body(st):
            kk, carries, accs = st
            kk = pl.multiple_of(kk, SB_SUB)
            new_c, new_a = [], []
            for h, hs in enumerate(heads):
                ls, hl = _sb_split(_sb_scores(qs[h], k_ref[0, pl.ds(kk, SB_SUB), hs]))
                w, c = _sb_weights(ls, hl, carries[h], mo_ref[...])
                new_c.append(c)
                new_a.append(accs[h] + jnp.dot(w, v_ref[0, pl.ds(kk, SB_SUB), hs],
                                               preferred_element_type=F32))
            return kk - SB_SUB, tuple(new_c), tuple(new_a)

        carries = tuple(call_s[i, h] for h in range(SB_HP))
        accs = tuple(o_ref[0, pl.ds(q0, T), hs].astype(F32) for hs in heads)
        _, _, accs = lax.while_loop(cond, body, (k0 - SB_SUB, carries, accs))
        for h, hs in enumerate(heads):
            o_ref[0, pl.ds(q0, T), hs] = accs[h].astype(BF16)

    def second_pass(i, carry):
        @pl.when(live_s[i] > SB_DEAD2)
        def _():
            continue_tile(i)
        return carry

    lax.fori_loop(SB_NSUB, n_q, second_pass, 0)


def _stickbreak(hb3):
    B, S, _ = hb3.shape
    wblk = SB_HP * SB_DH
    per_group = D_MODEL // wblk
    blk = lambda group: pl.BlockSpec((1, S, wblk), lambda b, g: (b, 0, group * per_group + g))
    const = lambda a: pl.BlockSpec(a.shape, lambda b, g: (0, 0))
    mo = jnp.asarray(_sb_suffix_matrix(), dtype=BF16)
    bias = jnp.asarray(_sb_window_bias())
    win = SB_NSUB * SB_SUB
    return pl.pallas_call(
        _sb_kernel,
        grid=(B, SB_HEADS // SB_HP),
        in_specs=[blk(HB_SQ), blk(HB_SK), blk(HB_SV), const(mo),
                  pl.BlockSpec(bias.shape, lambda b, g: (0, 0, 0))],
        out_specs=pl.BlockSpec((1, S, wblk), lambda b, g: (b, 0, g)),
        out_shape=jax.ShapeDtypeStruct((B, S, SB_HEADS * SB_DH), BF16),
        scratch_shapes=[pltpu.VMEM((2, SB_HP, SB_TQ, win), F32),
                        pltpu.VMEM((2, SB_HP, SB_TQ, win), F32),
                        pltpu.VMEM((2, SB_HP, SB_TQ, 2 * win), BF16),
                        pltpu.VMEM((2, SB_HP, SB_TQ, win), BF16),
                        pltpu.VMEM((2, SB_HP, SB_TQ, SB_SUB), F32),
                        pltpu.VMEM((S // SB_TQ, SB_HP, SB_TQ, SB_SUB), F32),
                        pltpu.SMEM((S // SB_TQ,), F32)],
        compiler_params=pltpu.CompilerParams(
            dimension_semantics=("arbitrary", "arbitrary"), vmem_limit_bytes=VMEM_LIMIT),
        name="stickbreak",
    )(hb3, hb3, hb3, mo, bias)


POST_TM = 512
FFN_BLK = 256
POST_ROWS = 128


def _post_kernel(x_ref, og_ref, os_ref, ma_ref, mb_ref, mod_e_ref, mod_l_ref, wout_ref, wg_ref,
                 wu_ref, wo_ref, g1_ref, b1_ref, g2_ref, b2_ref, o_ref, x1_s, u2_s, a_s):
    t = pl.program_id(0)

    @pl.when(t == 0)
    def _():
        x1_s[...] = jnp.zeros_like(x1_s)
        u2_s[...] = jnp.zeros_like(u2_s)

    n_ff = D_FF // FFN_BLK
    n_rows = x_ref.shape[0] // POST_ROWS

    def early_rows(slot, r):
        rs = slice(r * POST_ROWS, (r + 1) * POST_ROWS)
        merged = (_sigmoid(ma_ref[rs, :].astype(F32)) * og_ref[rs, :].astype(F32)
                  + _sigmoid(mb_ref[rs, :].astype(F32)) * os_ref[rs, :].astype(F32))
        y = jnp.dot(merged.astype(BF16), wout_ref[...], preferred_element_type=F32)
        gate1 = mod_e_ref[0, :, 2 * D_MODEL:3 * D_MODEL]
        shift2 = mod_e_ref[0, :, 3 * D_MODEL:4 * D_MODEL]
        scale2 = mod_e_ref[0, :, 4 * D_MODEL:5 * D_MODEL]
        x1 = _ln(ALPHA * x_ref[rs, :] + (1.0 + gate1) * y) * g1_ref[...] + b1_ref[...]
        x1_s[slot, rs, :] = x1
        u2_s[slot, rs, :] = (_ln(x1) * (1.0 + scale2) + shift2).astype(BF16)

    def ffn_block(slot, f):
        cols = slice(f * FFN_BLK, (f + 1) * FFN_BLK)
        u = u2_s[slot]
        g = jnp.dot(u, wg_ref[:, cols], preferred_element_type=F32)
        up = jnp.dot(u, wu_ref[:, cols], preferred_element_type=F32)
        a_s[:, cols] = (g * _sigmoid(g) * up).astype(BF16)

    def out_rows(slot, r):
        rs = slice(r * POST_ROWS, (r + 1) * POST_ROWS)
        ff = jnp.dot(a_s[rs, :], wo_ref[...], preferred_element_type=F32)
        gate2 = mod_l_ref[0, :, 5 * D_MODEL:6 * D_MODEL]
        o_ref[rs, :] = (_ln(ALPHA * x1_s[slot, rs, :] + (1.0 + gate2) * ff) * g2_ref[...]
                        + b2_ref[...])

    def step(e_slot, l_slot):
        every = n_ff // n_rows
        for f in range(n_ff):
            ffn_block(l_slot, f)
            if f % every == 0 and f // every < n_rows:
                early_rows(e_slot, f // every)
        for r in range(n_rows):
            out_rows(l_slot, r)

    @pl.when(t % 2 == 0)
    def _():
        step(0, 1)

    @pl.when(t % 2 == 1)
    def _():
        step(1, 0)


def _post(x2d, og2, os2, hb2, mod3, w_out, w_ffn_in, w_ffn_out, ln1_g, ln1_b, ln2_g, ln2_b, seq):
    M, D = x2d.shape
    tm = POST_TM
    n = M // tm
    per_b = seq // tm
    e_tile = lambda t: jnp.minimum(t, n - 1)
    l_tile = lambda t: jnp.maximum(t - 1, 0)
    once = pl.Buffered(1)
    row_e = lambda c: pl.BlockSpec((tm, D), lambda t: (e_tile(t), c))
    vec = pl.BlockSpec((1, D), lambda t: (0, 0), pipeline_mode=once)
    return pl.pallas_call(
        _post_kernel,
        grid=(n + 1,),
        in_specs=[row_e(0), row_e(0), row_e(0), row_e(HB_MA), row_e(HB_MB),
                  pl.BlockSpec((1, 1, 6 * D_MODEL), lambda t: (e_tile(t) // per_b, 0, 0)),
                  pl.BlockSpec((1, 1, 6 * D_MODEL), lambda t: (l_tile(t) // per_b, 0, 0)),
                  pl.BlockSpec((D, D), lambda t: (0, 0), pipeline_mode=once),
                  pl.BlockSpec((D, D_FF), lambda t: (0, 0), pipeline_mode=once),
                  pl.BlockSpec((D, D_FF), lambda t: (0, 1), pipeline_mode=once),
                  pl.BlockSpec((D_FF, D), lambda t: (0, 0), pipeline_mode=once),
                  vec, vec, vec, vec],
        out_specs=pl.BlockSpec((tm, D), lambda t: (l_tile(t), 0)),
        out_shape=jax.ShapeDtypeStruct((M, D), F32),
        scratch_shapes=[pltpu.VMEM((2, tm, D), F32),
                        pltpu.VMEM((2, tm, D), BF16),
                        pltpu.VMEM((tm, D_FF), BF16)],
        compiler_params=pltpu.CompilerParams(
            dimension_semantics=("arbitrary",), vmem_limit_bytes=VMEM_LIMIT),
        name="post",
    )(x2d, og2, os2, hb2, hb2, mod3, mod3, w_out, w_ffn_in, w_ffn_in, w_ffn_out,
      ln1_g.reshape(1, D), ln1_b.reshape(1, D), ln2_g.reshape(1, D), ln2_b.reshape(1, D))


def _regroup_w_in(w_in_l):
    o = 0
    parts = {}
    for name, wdt in (("gq", GLA_QK), ("gk", GLA_QK), ("gv", GLA_VW), ("gg", GLA_VW),
                      ("ga", GLA_LOWRANK), ("sq", D_MODEL), ("sk", D_MODEL), ("sv", D_MODEL),
                      ("ma", D_MODEL), ("mb", D_MODEL)):
        parts[name] = w_in_l[:, o:o + wdt]
        o += wdt
    w_all = jnp.concatenate(
        [parts["gq"], parts["gk"], parts["gv"], parts["gg"], parts["sq"],
         parts["sk"], parts["sv"], parts["ma"], parts["mb"]], axis=1).astype(BF16)
    return w_all, parts["ga"]


def _colscale():
    ones = lambda n: jnp.ones((n,), F32)
    return jnp.concatenate([
        ones(GLA_QK) * (GLA_DK ** -0.5), ones(GLA_QK), ones(2 * GLA_VW),
        ones(D_MODEL) * (SB_DH ** -0.5 * LOG2E), ones(4 * D_MODEL)]).reshape(1, HB_W)


def kernel(x, c, w_ada, b_ada, w_in, w_alpha_up, b_alpha, gla_norm_g, w_out,
           ln1_g, ln1_b, w_ffn_in, w_ffn_out, ln2_g, ln2_b):
    B, S, D = x.shape
    M = B * S
    x2d = x.reshape(M, D)
    for l in range(DEPTH):
        mod3 = _adaln_mod(c, w_ada[l], b_ada[l]).reshape(B, 1, 6 * D)
        w_all, w_ga = _regroup_w_in(w_in[l])
        w_decay = _decay_weight(w_ga, w_alpha_up[l]).astype(BF16)
        hb, xa = _in_proj(x2d, mod3, w_all, _colscale(), w_decay, S)
        hb3 = hb.reshape(B, S, HB_W)
        og = _gla(hb3, xa.reshape(B, S, GLA_QK), b_alpha[l], gla_norm_g[l])
        osb = _stickbreak(hb3)
        x2d = _post(x2d, og.reshape(M, D), osb.reshape(M, D), hb, mod3, w_out[l].astype(BF16),
                    w_ffn_in[l].astype(BF16), w_ffn_out[l].astype(BF16),
                    ln1_g[l], ln1_b[l], ln2_g[l], ln2_b[l], S)
    return x2d.reshape(B, S, D)
```

```python
import functools
import math

import numpy as np
import jax
import jax.numpy as jnp
from jax import lax
from jax.experimental import pallas as pl
from jax.experimental.pallas import tpu as pltpu

F32 = jnp.float32
BF16 = jnp.bfloat16

D_MODEL = 1024
GLA_HEADS = 4
GLA_DK = 128
GLA_DV = 256
GLA_QK = GLA_HEADS * GLA_DK
GLA_VW = GLA_HEADS * GLA_DV
GLA_LOWRANK = 16
GLA_TAU = 16.0
GLA_CHUNK = 64
SB_HEADS = 8
SB_DH = 128
D_FF = 2816
DEPTH = 1
ALPHA = (2.0 * DEPTH) ** 0.25
LN_EPS = 1e-5
LOG2E = math.log2(math.e)

HB_W = 8 * D_MODEL
IN_TN = 1024
HB_TILES = HB_W // IN_TN
HB_GQK, HB_GV, HB_GG, HB_SQ, HB_SK, HB_SV, HB_MA, HB_MB = 0, 1, 2, 3, 4, 5, 6, 7

VMEM_LIMIT = 56 * 1024 * 1024
LN_ROWS = 256


def _ln(x):
    mu = jnp.mean(x, axis=-1, keepdims=True)
    xc = x - mu
    var = jnp.mean(xc * xc, axis=-1, keepdims=True)
    return xc * lax.rsqrt(var + LN_EPS)


def _sigmoid(x):
    return 1.0 / (1.0 + jnp.exp(-x))


def _log2_sigmoid2(x2):
    return jnp.minimum(x2, 0.0) - jnp.log(1.0 + jnp.exp2(-jnp.abs(x2))) * LOG2E


def _adaln_kernel(c_ref, w_ref, b_ref, o_ref):
    c = c_ref[...]
    ca = c * _sigmoid(c)
    o_ref[...] = jnp.dot(ca, w_ref[...], preferred_element_type=F32,
                         precision=lax.Precision.HIGHEST) + b_ref[...]


def _adaln_mod(c, w_ada, b_ada):
    B, D = c.shape
    N = w_ada.shape[1]
    tn = 1024
    return pl.pallas_call(
        _adaln_kernel,
        grid=(N // tn,),
        in_specs=[pl.BlockSpec((B, D), lambda j: (0, 0)),
                  pl.BlockSpec((D, tn), lambda j: (0, j)),
                  pl.BlockSpec((1, tn), lambda j: (0, j))],
        out_specs=pl.BlockSpec((B, tn), lambda j: (0, j)),
        out_shape=jax.ShapeDtypeStruct((B, N), F32),
        name="adaln_mod",
    )(c, w_ada, b_ada.reshape(1, N))


def _decay_weight_kernel(wga_ref, wup_ref, o_ref):
    o_ref[...] = jnp.dot(wga_ref[...], wup_ref[...], preferred_element_type=F32,
                         precision=lax.Precision.HIGHEST)


def _decay_weight(w_ga, w_up):
    D = w_ga.shape[0]
    return pl.pallas_call(
        _decay_weight_kernel,
        out_shape=jax.ShapeDtypeStruct((D, GLA_QK), F32),
        name="decay_weight",
    )(w_ga, w_up)


def _inproj_kernel(x_ref, mod_ref, w_ref, cs_ref, wa_ref, hb_ref, xa_ref, u_ref):
    j = pl.program_id(1)

    @pl.when(j == 0)
    def _():
        shift = mod_ref[0, :, 0 * D_MODEL:1 * D_MODEL]
        scale1p = 1.0 + mod_ref[0, :, 1 * D_MODEL:2 * D_MODEL]

        def rows(t, carry):
            r0 = pl.multiple_of(t * LN_ROWS, LN_ROWS)
            u = _ln(x_ref[pl.ds(r0, LN_ROWS), :]) * scale1p + shift
            u_ref[pl.ds(r0, LN_ROWS), :] = u.astype(BF16)
            return carry

        lax.fori_loop(0, x_ref.shape[0] // LN_ROWS, rows, 0)
        xa_ref[...] = jnp.dot(u_ref[...], wa_ref[...], preferred_element_type=F32) * LOG2E

    acc = jnp.dot(u_ref[...], w_ref[...], preferred_element_type=F32) * cs_ref[...]
    hb_ref[...] = acc.astype(BF16)


def _in_proj(x2d, mod3, w_all, colscale, w_decay, seq):
    M, D = x2d.shape
    tm = 2048
    per_b = seq // tm
    return pl.pallas_call(
        _inproj_kernel,
        grid=(M // tm, HB_TILES),
        in_specs=[pl.BlockSpec((tm, D), lambda i, j: (i, 0)),
                  pl.BlockSpec((1, 1, 6 * D_MODEL), lambda i, j: (i // per_b, 0, 0)),
                  pl.BlockSpec((D, IN_TN), lambda i, j: (0, j)),
                  pl.BlockSpec((1, IN_TN), lambda i, j: (0, j)),
                  pl.BlockSpec((D, GLA_QK), lambda i, j: (0, 0))],
        out_specs=[pl.BlockSpec((tm, IN_TN), lambda i, j: (i, j)),
                   pl.BlockSpec((tm, GLA_QK), lambda i, j: (i, 0))],
        out_shape=[jax.ShapeDtypeStruct((M, HB_W), BF16),
                   jax.ShapeDtypeStruct((M, GLA_QK), F32)],
        scratch_shapes=[pltpu.VMEM((tm, D), BF16)],
        compiler_params=pltpu.CompilerParams(
            dimension_semantics=("arbitrary", "arbitrary"), vmem_limit_bytes=VMEM_LIMIT),
        name="in_proj",
    )(x2d, mod3, w_all, colscale, w_decay)


GLA_LEVELS = (32, 16, 8, 4, 2, 1)
GLA_FAST_MAX_DECAY = 60.0
GLA_FAST_MIN_X2 = 1.0 - GLA_FAST_MAX_DECAY * GLA_TAU / GLA_CHUNK


def _gla_cum_matrix():
    C = GLA_CHUNK
    r = np.arange(C)[:, None]
    s = np.arange(C)[None, :]
    blocks = [s <= r]
    for blk in GLA_LEVELS:
        blocks.append(s <= (r // (2 * blk)) * (2 * blk) + blk - 1)
    one = np.concatenate(blocks, axis=0).astype(np.float32)
    return np.concatenate([one, one, one], axis=1)


def _gla_pair_masks():
    C = GLA_CHUNK
    r = np.arange(C)[:, None]
    s = np.arange(C)[None, :]
    masks = [r == s]
    for blk in GLA_LEVELS:
        masks.append(((r // blk) == (s // blk) + 1) & ((r // (2 * blk)) == (s // (2 * blk))))
    masks.append(s <= r)
    return np.stack(masks).astype(np.float32)


def _gla_kernel(q_ref, k_ref, xa_ref, v_ref, gg_ref, ba_ref, gn_ref, pm_ref, cum_ref, o_ref,
                state_ref, b_s, ql_s, kl_s, p_s, qe_s, kd_s, dec_s, *, ts):
    C = GLA_CHUNK
    n_levels = len(GLA_LEVELS)
    n_chunks = ts // C
    nt = (((1,), (1,)), ((), ()))
    kslices = [slice(h * GLA_DK, (h + 1) * GLA_DK) for h in range(GLA_HEADS)]
    vslices = [slice(h * GLA_DV, (h + 1) * GLA_DV) for h in range(GLA_HEADS)]

    @pl.when(pl.program_id(1) == 0)
    def _():
        state_ref[...] = jnp.zeros_like(state_ref)

    ba2 = ba_ref[...] * LOG2E

    def rows(n):
        return pl.ds(n * C if isinstance(n, int) else pl.multiple_of(n * C, C), C)

    def decay_stage(n, slot, fast):
        la = _log2_sigmoid2(xa_ref[0, rows(n), :] + ba2) * (1.0 / GLA_TAU)
        la_hi = la.astype(BF16)
        rem = la - la_hi.astype(F32)
        la_mid = rem.astype(BF16)
        la_lo = (rem - la_mid.astype(F32)).astype(BF16)
        la3 = jnp.concatenate([la_hi, la_mid, la_lo], axis=0)
        if fast:
            ball = jnp.dot(cum_ref[0:C, :], la3, preferred_element_type=F32)
            for h, ks in enumerate(kslices):
                b_s[slot, h] = ball[:, ks]
            return
        ball = jnp.dot(cum_ref[...], la3, preferred_element_type=F32)
        q = q_ref[0, rows(n), :].astype(F32)
        k = k_ref[0, rows(n), :].astype(F32)
        for h, ks in enumerate(kslices):
            b = ball[0:C, ks]
            b_s[slot, h] = b
            for li in range(n_levels):
                f = jnp.exp2(-jnp.abs(b - ball[(li + 1) * C:(li + 2) * C, ks]))
                ql_s[slot, h, li] = (q[:, ks] * f).astype(BF16)
                kl_s[slot, h, li] = (k[:, ks] * f).astype(BF16)

    def score_stage(n, slot, fast):
        q16 = q_ref[0, rows(n), :]
        k16 = k_ref[0, rows(n), :]
        q = q16.astype(F32)
        k = k16.astype(F32)
        for h, ks in enumerate(kslices):
            b = b_s[slot, h]
            b_last = b[C - 1:C, :]
            qe = (q[:, ks] * jnp.exp2(b)).astype(BF16)
            if fast:
                ki = (k[:, ks] * jnp.exp2(-b)).astype(BF16)
                sc = lax.dot_general(qe, ki, nt, preferred_element_type=F32)
                p = jnp.where(pm_ref[n_levels + 1] != 0.0, sc, 0.0)
            else:
                sc = lax.dot_general(q16[:, ks], k16[:, ks], nt, preferred_element_type=F32)
                p = jnp.where(pm_ref[0] != 0.0, sc, 0.0)
                for li in range(n_levels):
                    sc = lax.dot_general(ql_s[slot, h, li], kl_s[slot, h, li], nt,
                                         preferred_element_type=F32)
                    p = jnp.where(pm_ref[li + 1] != 0.0, sc, p)
            p_s[slot, h] = p.astype(BF16)
            qe_s[slot, h] = qe
            kd_s[slot, h] = (k[:, ks] * jnp.exp2(b_last - b)).astype(BF16)
            dec_s[slot, h] = jnp.exp2(b_last)

    def output_stage(n, slot):
        v = v_ref[0, rows(n), :]
        gg = gg_ref[0, rows(n), :].astype(F32)
        for h, vs in enumerate(vslices):
            st = state_ref[h]
            vh = v[:, vs]
            o = (lax.dot_general(qe_s[slot, h], st.astype(BF16), nt, preferred_element_type=F32)
                 + jnp.dot(p_s[slot, h], vh, preferred_element_type=F32))
            upd = lax.dot_general(vh, kd_s[slot, h], (((0,), (0,)), ((), ())),
                                  preferred_element_type=F32)
            state_ref[h] = st * dec_s[slot, h] + upd
            on = o * lax.rsqrt(jnp.mean(o * o, axis=-1, keepdims=True) + LN_EPS)
            g = gg[:, vs]
            on = on * gn_ref[:, vs] * (g * _sigmoid(g))
            o_ref[0, rows(n), vs] = on.astype(BF16)

    def run(fast):
        decay_stage(0, 0, fast)
        decay_stage(1, 1, fast)
        score_stage(0, 0, fast)

        def pair(p, carry):
            n = 2 * p
            decay_stage(n + 2, 0, fast)
            score_stage(n + 1, 1, fast)
            output_stage(n, 0)
            decay_stage(n + 3, 1, fast)
            score_stage(n + 2, 0, fast)
            output_stage(n + 1, 1)
            return carry

        assert n_chunks % 2 == 0
        lax.fori_loop(0, (n_chunks - 2) // 2, pair, 0)
        score_stage(n_chunks - 1, 1, fast)
        output_stage(n_chunks - 2, 0)
        output_stage(n_chunks - 1, 1)

    safe = jnp.min(xa_ref[0] + ba2) >= GLA_FAST_MIN_X2

    @pl.when(safe)
    def _():
        run(True)

    @pl.when(jnp.logical_not(safe))
    def _():
        run(False)


def _gla(hb3, xa3, b_alpha, gla_norm_g):
    B, S, _ = hb3.shape
    ts = 1024
    kern = functools.partial(_gla_kernel, ts=ts)
    n_masks = 2 + len(GLA_LEVELS)
    H, C, L = GLA_HEADS, GLA_CHUNK, len(GLA_LEVELS)
    return pl.pallas_call(
        kern,
        grid=(B, S // ts),
        in_specs=[pl.BlockSpec((1, ts, GLA_QK), lambda b, s: (b, s, 2 * HB_GQK)),
                  pl.BlockSpec((1, ts, GLA_QK), lambda b, s: (b, s, 2 * HB_GQK + 1)),
                  pl.BlockSpec((1, ts, GLA_QK), lambda b, s: (b, s, 0)),
                  pl.BlockSpec((1, ts, GLA_VW), lambda b, s: (b, s, HB_GV)),
                  pl.BlockSpec((1, ts, GLA_VW), lambda b, s: (b, s, HB_GG)),
                  pl.BlockSpec((1, GLA_QK), lambda b, s: (0, 0)),
                  pl.BlockSpec((1, GLA_VW), lambda b, s: (0, 0)),
                  pl.BlockSpec((n_masks, C, C), lambda b, s: (0, 0, 0)),
                  pl.BlockSpec(((1 + L) * C, 3 * C), lambda b, s: (0, 0))],
        out_specs=pl.BlockSpec((1, ts, GLA_VW), lambda b, s: (b, s, 0)),
        out_shape=jax.ShapeDtypeStruct((B, S, GLA_VW), BF16),
        scratch_shapes=[pltpu.VMEM((H, GLA_DV, GLA_DK), F32),
                        pltpu.VMEM((2, H, C, GLA_DK), F32),
                        pltpu.VMEM((2, H, L, C, GLA_DK), BF16),
                        pltpu.VMEM((2, H, L, C, GLA_DK), BF16),
                        pltpu.VMEM((2, H, C, C), BF16),
                        pltpu.VMEM((2, H, C, GLA_DK), BF16),
                        pltpu.VMEM((2, H, C, GLA_DK), BF16),
                        pltpu.VMEM((2, H, 1, GLA_DK), F32)],
        compiler_params=pltpu.CompilerParams(
            dimension_semantics=("arbitrary", "arbitrary"), vmem_limit_bytes=VMEM_LIMIT),
        name="gla",
    )(hb3, hb3, xa3, hb3, hb3, b_alpha.reshape(1, GLA_QK), gla_norm_g.reshape(1, GLA_VW),
      jnp.asarray(_gla_pair_masks()), jnp.asarray(_gla_cum_matrix(), dtype=BF16))


SB_TQ = 128
SB_SUB = 128
SB_NSUB = 3
SB_HP = 4
SB_MASKED = -1e30
SB_DEAD2 = -152.0


def _sb_suffix_matrix():
    j = np.arange(2 * SB_SUB)[:, None] % SB_SUB
    s = np.arange(2 * SB_SUB)[None, :]
    return ((s >= SB_SUB) | (j > s)).astype(np.float32)


def _sb_window_bias():
    r = np.arange(SB_TQ)[:, None]
    c = np.arange(SB_NSUB * SB_SUB)[None, :]
    return np.stack([np.where(c < r + d * SB_SUB, 0.0, SB_MASKED)
                     for d in range(SB_NSUB)]).astype(np.float32)


def _sb_scores(q, k):
    return lax.dot_general(q, k, (((1,), (1,)), ((), ())), preferred_element_type=F32)


def _sb_split(z):
    n_sub = z.shape[1] // SB_SUB
    ls = _log2_sigmoid2(z)
    lk = ls - z
    hi = lk.astype(BF16)
    lo = (lk - hi.astype(F32)).astype(BF16)
    parts = []
    for c in range(n_sub):
        cs = slice(c * SB_SUB, (c + 1) * SB_SUB)
        parts += [hi[:, cs], lo[:, cs]]
    return ls, jnp.concatenate(parts, axis=1)


def _sb_weights(ls, hl, carry, mo):
    n_sub = ls.shape[1] // SB_SUB
    ws = [None] * n_sub
    for c in reversed(range(n_sub)):
        r = jnp.dot(hl[:, 2 * c * SB_SUB:2 * (c + 1) * SB_SUB], mo, preferred_element_type=F32)
        ws[c] = jnp.exp2(ls[:, c * SB_SUB:(c + 1) * SB_SUB] + (r[:, :SB_SUB] + carry)).astype(BF16)
        carry = carry + r[:, SB_SUB:]
    return (jnp.concatenate(ws, axis=1) if n_sub > 1 else ws[0]), carry


def _sb_kernel(q_ref, k_ref, v_ref, mo_ref, bias_ref, o_ref, z_s, ls_s, hl_s, w_s, c_s, call_s,
               live_s):
    T = SB_TQ
    W = SB_NSUB * SB_SUB
    n_q = q_ref.shape[1] // T
    heads = [slice(h * SB_DH, (h + 1) * SB_DH) for h in range(SB_HP)]
    zero = jnp.zeros((T, SB_SUB), F32)

    def row0(i):
        return i * T if isinstance(i, int) else pl.multiple_of(i * T, T)

    def window(i):
        if isinstance(i, int):
            return max(i + 1 - SB_NSUB, 0) * T, min(i, SB_NSUB - 1)
        return row0(jnp.maximum(i + 1 - SB_NSUB, 0)), jnp.minimum(i, SB_NSUB - 1)

    def scores_stage(i, slot):
        q0 = row0(i)
        k0, d = window(i)
        for h, hs in enumerate(heads):
            z = _sb_scores(q_ref[0, pl.ds(q0, T), hs], k_ref[0, pl.ds(k0, W), hs])
            z_s[slot, h] = z + bias_ref[d]

    def split_stage(slot):
        for h in range(SB_HP):
            ls, hl = _sb_split(z_s[slot, h])
            ls_s[slot, h] = ls
            hl_s[slot, h] = hl

    def weights_stage(i, slot):
        for h in range(SB_HP):
            w, c = _sb_weights(ls_s[slot, h], hl_s[slot, h], zero, mo_ref[...])
            w_s[slot, h] = w
            c_s[slot, h] = c
            call_s[i, h] = c

    def output_stage(i, slot):
        q0 = row0(i)
        k0, _ = window(i)
        for h, hs in enumerate(heads):
            acc = jnp.dot(w_s[slot, h], v_ref[0, pl.ds(k0, W), hs], preferred_element_type=F32)
            o_ref[0, pl.ds(q0, T), hs] = acc.astype(BF16)
        live = functools.reduce(jnp.maximum, [c_s[slot, h] for h in range(SB_HP)])
        live_s[i] = jnp.max(live)

    def steady(j, s):
        scores_stage(j, s)
        split_stage(1 - s)
        weights_stage(j - 2, s)
        output_stage(j - 3, 1 - s)

    scores_stage(0, 0)
    scores_stage(1, 1)
    split_stage(0)
    scores_stage(2, 0)
    split_stage(1)
    weights_stage(0, 0)

    def pair(p, carry):
        j = 3 + 2 * p
        steady(j, 1)
        steady(j + 1, 0)
        return carry

    n_pairs = (n_q - 3) // 2
    lax.fori_loop(0, n_pairs, pair, 0)
    for j in range(3 + 2 * n_pairs, n_q):
        steady(j, j % 2)
    last = (n_q - 1) % 2
    split_stage(last)
    weights_stage(n_q - 2, 1 - last)
    output_stage(n_q - 3, last)
    weights_stage(n_q - 1, last)
    output_stage(n_q - 2, 1 - last)
    output_stage(n_q - 1, last)

    def continue_tile(i):
        q0 = row0(i)
        k0, _ = window(i)
        qs = [q_ref[0, pl.ds(q0, T), hs] for hs in heads]

        def cond(st):
            kk, carries, _ = st
            live = functools.reduce(jnp.maximum, carries)
            return jnp.logical_and(kk >= 0, jnp.max(live) > SB_DEAD2)

        def body(st):
            kk, carries, accs = st
            kk = pl.multiple_of(kk, SB_SUB)
            new_c, new_a = [], []
            for h, hs in enumerate(heads):
                ls, hl = _sb_split(_sb_scores(qs[h], k_ref[0, pl.ds(kk, SB_SUB), hs]))
                w, c = _sb_weights(ls, hl, carries[h], mo_ref[...])
                new_c.append(c)
                new_a.append(accs[h] + jnp.dot(w, v_ref[0, pl.ds(kk, SB_SUB), hs],
                                               preferred_element_type=F32))
            return kk - SB_SUB, tuple(new_c), tuple(new_a)

        carries = tuple(call_s[i, h] for h in range(SB_HP))
        accs = tuple(o_ref[0, pl.ds(q0, T), hs].astype(F32) for hs in heads)
        _, _, accs = lax.while_loop(cond, body, (k0 - SB_SUB, carries, accs))
        for h, hs in enumerate(heads):
            o_ref[0, pl.ds(q0, T), hs] = accs[h].astype(BF16)

    def second_pass(i, carry):
        @pl.when(live_s[i] > SB_DEAD2)
        def _():
            continue_tile(i)
        return carry

    lax.fori_loop(SB_NSUB, n_q, second_pass, 0)


def _stickbreak(hb3):
    B, S, _ = hb3.shape
    wblk = SB_HP * SB_DH
    per_group = D_MODEL // wblk
    blk = lambda group: pl.BlockSpec((1, S, wblk), lambda b, g: (b, 0, group * per_group + g))
    const = lambda a: pl.BlockSpec(a.shape, lambda b, g: (0, 0))
    mo = jnp.asarray(_sb_suffix_matrix(), dtype=BF16)
    bias = jnp.asarray(_sb_window_bias())
    win = SB_NSUB * SB_SUB
    return pl.pallas_call(
        _sb_kernel,
        grid=(B, SB_HEADS // SB_HP),
        in_specs=[blk(HB_SQ), blk(HB_SK), blk(HB_SV), const(mo),
                  pl.BlockSpec(bias.shape, lambda b, g: (0, 0, 0))],
        out_specs=pl.BlockSpec((1, S, wblk), lambda b, g: (b, 0, g)),
        out_shape=jax.ShapeDtypeStruct((B, S, SB_HEADS * SB_DH), BF16),
        scratch_shapes=[pltpu.VMEM((2, SB_HP, SB_TQ, win), F32),
                        pltpu.VMEM((2, SB_HP, SB_TQ, win), F32),
                        pltpu.VMEM((2, SB_HP, SB_TQ, 2 * win), BF16),
                        pltpu.VMEM((2, SB_HP, SB_TQ, win), BF16),
                        pltpu.VMEM((2, SB_HP, SB_TQ, SB_SUB), F32),
                        pltpu.VMEM((S // SB_TQ, SB_HP, SB_TQ, SB_SUB), F32),
                        pltpu.SMEM((S // SB_TQ,), F32)],
        compiler_params=pltpu.CompilerParams(
            dimension_semantics=("arbitrary", "arbitrary"), vmem_limit_bytes=VMEM_LIMIT),
        name="stickbreak",
    )(hb3, hb3, hb3, mo, bias)


POST_TM = 512
FFN_BLK = 256
POST_ROWS = 128


def _post_kernel(x_ref, og_ref, os_ref, ma_ref, mb_ref, mod_e_ref, mod_l_ref, wout_ref, wg_ref,
                 wu_ref, wo_ref, g1_ref, b1_ref, g2_ref, b2_ref, o_ref, x1_s, u2_s, a_s):
    t = pl.program_id(0)

    @pl.when(t == 0)
    def _():
        x1_s[...] = jnp.zeros_like(x1_s)
        u2_s[...] = jnp.zeros_like(u2_s)

    n_ff = D_FF // FFN_BLK
    n_rows = x_ref.shape[0] // POST_ROWS

    def early_rows(slot, r):
        rs = slice(r * POST_ROWS, (r + 1) * POST_ROWS)
        merged = (_sigmoid(ma_ref[rs, :].astype(F32)) * og_ref[rs, :].astype(F32)
                  + _sigmoid(mb_ref[rs, :].astype(F32)) * os_ref[rs, :].astype(F32))
        y = jnp.dot(merged.astype(BF16), wout_ref[...], preferred_element_type=F32)
        gate1 = mod_e_ref[0, :, 2 * D_MODEL:3 * D_MODEL]
        shift2 = mod_e_ref[0, :, 3 * D_MODEL:4 * D_MODEL]
        scale2 = mod_e_ref[0, :, 4 * D_MODEL:5 * D_MODEL]
        x1 = _ln(ALPHA * x_ref[rs, :] + (1.0 + gate1) * y) * g1_ref[...] + b1_ref[...]
        x1_s[slot, rs, :] = x1
        u2_s[slot, rs, :] = (_ln(x1) * (1.0 + scale2) + shift2).astype(BF16)

    def ffn_block(slot, f):
        cols = slice(f * FFN_BLK, (f + 1) * FFN_BLK)
        u = u2_s[slot]
        g = jnp.dot(u, wg_ref[:, cols], preferred_element_type=F32)
        up = jnp.dot(u, wu_ref[:, cols], preferred_element_type=F32)
        a_s[:, cols] = (g * _sigmoid(g) * up).astype(BF16)

    def out_rows(slot, r):
        rs = slice(r * POST_ROWS, (r + 1) * POST_ROWS)
        ff = jnp.dot(a_s[rs, :], wo_ref[...], preferred_element_type=F32)
        gate2 = mod_l_ref[0, :, 5 * D_MODEL:6 * D_MODEL]
        o_ref[rs, :] = (_ln(ALPHA * x1_s[slot, rs, :] + (1.0 + gate2) * ff) * g2_ref[...]
                        + b2_ref[...])

    def step(e_slot, l_slot):
        every = n_ff // n_rows
        for f in range(n_ff):
            ffn_block(l_slot, f)
            if f % every == 0 and f // every < n_rows:
                early_rows(e_slot, f // every)
        for r in range(n_rows):
            out_rows(l_slot, r)

    @pl.when(t % 2 == 0)
    def _():
        step(0, 1)

    @pl.when(t % 2 == 1)
    def _():
        step(1, 0)


def _post(x2d, og2, os2, hb2, mod3, w_out, w_ffn_in, w_ffn_out, ln1_g, ln1_b, ln2_g, ln2_b, seq):
    M, D = x2d.shape
    tm = POST_TM
    n = M // tm
    per_b = seq // tm
    e_tile = lambda t: jnp.minimum(t, n - 1)
    l_tile = lambda t: jnp.maximum(t - 1, 0)
    once = pl.Buffered(1)
    row_e = lambda c: pl.BlockSpec((tm, D), lambda t: (e_tile(t), c))
    vec = pl.BlockSpec((1, D), lambda t: (0, 0), pipeline_mode=once)
    return pl.pallas_call(
        _post_kernel,
        grid=(n + 1,),
        in_specs=[row_e(0), row_e(0), row_e(0), row_e(HB_MA), row_e(HB_MB),
                  pl.BlockSpec((1, 1, 6 * D_MODEL), lambda t: (e_tile(t) // per_b, 0, 0)),
                  pl.BlockSpec((1, 1, 6 * D_MODEL), lambda t: (l_tile(t) // per_b, 0, 0)),
                  pl.BlockSpec((D, D), lambda t: (0, 0), pipeline_mode=once),
                  pl.BlockSpec((D, D_FF), lambda t: (0, 0), pipeline_mode=once),
                  pl.BlockSpec((D, D_FF), lambda t: (0, 1), pipeline_mode=once),
                  pl.BlockSpec((D_FF, D), lambda t: (0, 0), pipeline_mode=once),
                  vec, vec, vec, vec],
        out_specs=pl.BlockSpec((tm, D), lambda t: (l_tile(t), 0)),
        out_shape=jax.ShapeDtypeStruct((M, D), F32),
        scratch_shapes=[pltpu.VMEM((2, tm, D), F32),
                        pltpu.VMEM((2, tm, D), BF16),
                        pltpu.VMEM((tm, D_FF), BF16)],
        compiler_params=pltpu.CompilerParams(
            dimension_semantics=("arbitrary",), vmem_limit_bytes=VMEM_LIMIT),
        name="post",
    )(x2d, og2, os2, hb2, hb2, mod3, mod3, w_out, w_ffn_in, w_ffn_in, w_ffn_out,
      ln1_g.reshape(1, D), ln1_b.reshape(1, D), ln2_g.reshape(1, D), ln2_b.reshape(1, D))


def _regroup_w_in(w_in_l):
    o = 0
    parts = {}
    for name, wdt in (("gq", GLA_QK), ("gk", GLA_QK), ("gv", GLA_VW), ("gg", GLA_VW),
                      ("ga", GLA_LOWRANK), ("sq", D_MODEL), ("sk", D_MODEL), ("sv", D_MODEL),
                      ("ma", D_MODEL), ("mb", D_MODEL)):
        parts[name] = w_in_l[:, o:o + wdt]
        o += wdt
    w_all = jnp.concatenate(
        [parts["gq"], parts["gk"], parts["gv"], parts["gg"], parts["sq"],
         parts["sk"], parts["sv"], parts["ma"], parts["mb"]], axis=1).astype(BF16)
    return w_all, parts["ga"]


def _colscale():
    ones = lambda n: jnp.ones((n,), F32)
    return jnp.concatenate([
        ones(GLA_QK) * (GLA_DK ** -0.5), ones(GLA_QK), ones(2 * GLA_VW),
        ones(D_MODEL) * (SB_DH ** -0.5 * LOG2E), ones(4 * D_MODEL)]).reshape(1, HB_W)


def kernel(x, c, w_ada, b_ada, w_in, w_alpha_up, b_alpha, gla_norm_g, w_out,
           ln1_g, ln1_b, w_ffn_in, w_ffn_out, ln2_g, ln2_b):
    B, S, D = x.shape
    M = B * S
    x2d = x.reshape(M, D)
    for l in range(DEPTH):
        mod3 = _adaln_mod(c, w_ada[l], b_ada[l]).reshape(B, 1, 6 * D)
        w_all, w_ga = _regroup_w_in(w_in[l])
        w_decay = _decay_weight(w_ga, w_alpha_up[l]).astype(BF16)
        hb, xa = _in_proj(x2d, mod3, w_all, _colscale(), w_decay, S)
        hb3 = hb.reshape(B, S, HB_W)
        og = _gla(hb3, xa.reshape(B, S, GLA_QK), b_alpha[l], gla_norm_g[l])
        osb = _stickbreak(hb3)
        x2d = _post(x2d, og.reshape(M, D), osb.reshape(M, D), hb, mod3, w_out[l].astype(BF16),
                    w_ffn_in[l].astype(BF16), w_ffn_out[l].astype(BF16),
                    ln1_g[l], ln1_b[l], ln2_g[l], ln2_b[l], S)
    return x2d.reshape(B, S, D)
```

```python
import functools
import math

import numpy as np
import jax
import jax.numpy as jnp
from jax import lax
from jax.experimental import pallas as pl
from jax.experimental.pallas import tpu as pltpu

F32 = jnp.float32
BF16 = jnp.bfloat16

D_MODEL = 1024
GLA_HEADS = 4
GLA_DK = 128
GLA_DV = 256
GLA_QK = GLA_HEADS * GLA_DK
GLA_VW = GLA_HEADS * GLA_DV
GLA_LOWRANK = 16
GLA_TAU = 16.0
GLA_CHUNK = 64
SB_HEADS = 8
SB_DH = 128
D_FF = 2816
DEPTH = 1
ALPHA = (2.0 * DEPTH) ** 0.25
LN_EPS = 1e-5
LOG2E = math.log2(math.e)

HB_W = 8 * D_MODEL
IN_TM = 512
IN_TN = 1024
HB_GQK, HB_GV, HB_GG, HB_SQ, HB_SK, HB_SV, HB_MA, HB_MB = 0, 1, 2, 3, 4, 5, 6, 7

VMEM_LIMIT = 56 * 1024 * 1024
LN_ROWS = 128


def _ln(x):
    mu = jnp.mean(x, axis=-1, keepdims=True)
    xc = x - mu
    var = jnp.mean(xc * xc, axis=-1, keepdims=True)
    return xc * lax.rsqrt(var + LN_EPS)


def _sigmoid(x):
    return 1.0 / (1.0 + jnp.exp(-x))


def _log2_sigmoid2(x2):
    return jnp.minimum(x2, 0.0) - jnp.log(1.0 + jnp.exp2(-jnp.abs(x2))) * LOG2E


def _adaln_kernel(c_ref, w_ref, b_ref, o_ref):
    c = c_ref[...]
    ca = c * _sigmoid(c)
    o_ref[...] = jnp.dot(ca, w_ref[...], preferred_element_type=F32,
                         precision=lax.Precision.HIGHEST) + b_ref[...]


def _adaln_mod(c, w_ada, b_ada):
    B, D = c.shape
    N = w_ada.shape[1]
    tn = 1024
    return pl.pallas_call(
        _adaln_kernel,
        grid=(N // tn,),
        in_specs=[pl.BlockSpec((B, D), lambda j: (0, 0)),
                  pl.BlockSpec((D, tn), lambda j: (0, j)),
                  pl.BlockSpec((1, tn), lambda j: (0, j))],
        out_specs=pl.BlockSpec((B, tn), lambda j: (0, j)),
        out_shape=jax.ShapeDtypeStruct((B, N), F32),
        name="adaln_mod",
    )(c, w_ada, b_ada.reshape(1, N))


def _decay_weight_kernel(wga_ref, wup_ref, o_ref):
    o_ref[...] = jnp.dot(wga_ref[...], wup_ref[...], preferred_element_type=F32,
                         precision=lax.Precision.HIGHEST)


def _decay_weight(w_ga, w_up):
    D = w_ga.shape[0]
    return pl.pallas_call(
        _decay_weight_kernel,
        out_shape=jax.ShapeDtypeStruct((D, GLA_QK), F32),
        name="decay_weight",
    )(w_ga, w_up)


def _inproj_kernel(x_ref, mod_ref, w_ref, cs_ref, wa_ref, hb_ref, xa_ref, u_s):
    t = pl.program_id(0)
    n_col = HB_W // IN_TN
    n_rows = x_ref.shape[0] // LN_ROWS

    @pl.when(t == 0)
    def _():
        u_s[...] = jnp.zeros_like(u_s)

    def ln_rows(slot, r):
        rs = slice(r * LN_ROWS, (r + 1) * LN_ROWS)
        shift = mod_ref[0, :, 0 * D_MODEL:1 * D_MODEL]
        scale1p = 1.0 + mod_ref[0, :, 1 * D_MODEL:2 * D_MODEL]
        u_s[slot, rs, :] = (_ln(x_ref[rs, :]) * scale1p + shift).astype(BF16)

    def step(ln_slot, mm_slot):
        u = u_s[mm_slot]
        for c in range(n_col):
            cols = slice(c * IN_TN, (c + 1) * IN_TN)
            acc = jnp.dot(u, w_ref[:, cols], preferred_element_type=F32) * cs_ref[:, cols]
            hb_ref[:, cols] = acc.astype(BF16)
            if c < n_rows:
                ln_rows(ln_slot, c)
        xa_ref[...] = jnp.dot(u, wa_ref[...], preferred_element_type=F32) * LOG2E

    @pl.when(t % 2 == 0)
    def _():
        step(0, 1)

    @pl.when(t % 2 == 1)
    def _():
        step(1, 0)


def _in_proj(x2d, mod3, w_all, colscale, w_decay, seq):
    M, D = x2d.shape
    tm = IN_TM
    n = M // tm
    per_b = seq // tm
    assert tm // LN_ROWS <= HB_W // IN_TN
    ln_tile = lambda t: jnp.minimum(t, n - 1)
    mm_tile = lambda t: jnp.maximum(t - 1, 0)
    once = pl.Buffered(1)
    return pl.pallas_call(
        _inproj_kernel,
        grid=(n + 1,),
        in_specs=[pl.BlockSpec((tm, D), lambda t: (ln_tile(t), 0)),
                  pl.BlockSpec((1, 1, 6 * D_MODEL), lambda t: (ln_tile(t) // per_b, 0, 0)),
                  pl.BlockSpec((D, HB_W), lambda t: (0, 0), pipeline_mode=once),
                  pl.BlockSpec((1, HB_W), lambda t: (0, 0), pipeline_mode=once),
                  pl.BlockSpec((D, GLA_QK), lambda t: (0, 0), pipeline_mode=once)],
        out_specs=[pl.BlockSpec((tm, HB_W), lambda t: (mm_tile(t), 0)),
                   pl.BlockSpec((tm, GLA_QK), lambda t: (mm_tile(t), 0))],
        out_shape=[jax.ShapeDtypeStruct((M, HB_W), BF16),
                   jax.ShapeDtypeStruct((M, GLA_QK), F32)],
        scratch_shapes=[pltpu.VMEM((2, tm, D), BF16)],
        compiler_params=pltpu.CompilerParams(
            dimension_semantics=("arbitrary",), vmem_limit_bytes=VMEM_LIMIT),
        name="in_proj",
    )(x2d, mod3, w_all, colscale, w_decay)


GLA_LEVELS = (32, 16, 8, 4, 2, 1)
GLA_FAST_MAX_DECAY = 60.0
GLA_FAST_MIN_X2 = 1.0 - GLA_FAST_MAX_DECAY * GLA_TAU / GLA_CHUNK


def _gla_cum_matrix():
    C = GLA_CHUNK
    r = np.arange(C)[:, None]
    s = np.arange(C)[None, :]
    blocks = [s <= r]
    for blk in GLA_LEVELS:
        blocks.append(s <= (r // (2 * blk)) * (2 * blk) + blk - 1)
    one = np.concatenate(blocks, axis=0).astype(np.float32)
    return np.concatenate([one, one, one], axis=1)


def _gla_pair_masks():
    C = GLA_CHUNK
    r = np.arange(C)[:, None]
    s = np.arange(C)[None, :]
    masks = [r == s]
    for blk in GLA_LEVELS:
        masks.append(((r // blk) == (s // blk) + 1) & ((r // (2 * blk)) == (s // (2 * blk))))
    masks.append(s <= r)
    return np.stack(masks).astype(np.float32)


def _gla_kernel(q_ref, k_ref, xa_ref, v_ref, gg_ref, ba_ref, gn_ref, pm_ref, cum_ref, o_ref,
                state_ref, b_s, ql_s, kl_s, p_s, qe_s, kd_s, dec_s, *, ts):
    C = GLA_CHUNK
    n_levels = len(GLA_LEVELS)
    n_chunks = ts // C
    nt = (((1,), (1,)), ((), ()))
    kslices = [slice(h * GLA_DK, (h + 1) * GLA_DK) for h in range(GLA_HEADS)]
    vslices = [slice(h * GLA_DV, (h + 1) * GLA_DV) for h in range(GLA_HEADS)]

    @pl.when(pl.program_id(1) == 0)
    def _():
        state_ref[...] = jnp.zeros_like(state_ref)

    ba2 = ba_ref[...] * LOG2E

    def rows(n):
        return pl.ds(n * C if isinstance(n, int) else pl.multiple_of(n * C, C), C)

    def decay_stage(n, slot, fast):
        la = _log2_sigmoid2(xa_ref[0, rows(n), :] + ba2) * (1.0 / GLA_TAU)
        la_hi = la.astype(BF16)
        rem = la - la_hi.astype(F32)
        la_mid = rem.astype(BF16)
        la_lo = (rem - la_mid.astype(F32)).astype(BF16)
        la3 = jnp.concatenate([la_hi, la_mid, la_lo], axis=0)
        if fast:
            ball = jnp.dot(cum_ref[0:C, :], la3, preferred_element_type=F32)
            for h, ks in enumerate(kslices):
                b_s[slot, h] = ball[:, ks]
            return
        ball = jnp.dot(cum_ref[...], la3, preferred_element_type=F32)
        q = q_ref[0, rows(n), :].astype(F32)
        k = k_ref[0, rows(n), :].astype(F32)
        for h, ks in enumerate(kslices):
            b = ball[0:C, ks]
            b_s[slot, h] = b
            for li in range(n_levels):
                f = jnp.exp2(-jnp.abs(b - ball[(li + 1) * C:(li + 2) * C, ks]))
                ql_s[slot, h, li] = (q[:, ks] * f).astype(BF16)
                kl_s[slot, h, li] = (k[:, ks] * f).astype(BF16)

    def score_stage(n, slot, fast):
        q16 = q_ref[0, rows(n), :]
        k16 = k_ref[0, rows(n), :]
        q = q16.astype(F32)
        k = k16.astype(F32)
        for h, ks in enumerate(kslices):
            b = b_s[slot, h]
            b_last = b[C - 1:C, :]
            qe = (q[:, ks] * jnp.exp2(b)).astype(BF16)
            if fast:
                ki = (k[:, ks] * jnp.exp2(-b)).astype(BF16)
                sc = lax.dot_general(qe, ki, nt, preferred_element_type=F32)
                p = jnp.where(pm_ref[n_levels + 1] != 0.0, sc, 0.0)
            else:
                sc = lax.dot_general(q16[:, ks], k16[:, ks], nt, preferred_element_type=F32)
                p = jnp.where(pm_ref[0] != 0.0, sc, 0.0)
                for li in range(n_levels):
                    sc = lax.dot_general(ql_s[slot, h, li], kl_s[slot, h, li], nt,
                                         preferred_element_type=F32)
                    p = jnp.where(pm_ref[li + 1] != 0.0, sc, p)
            p_s[slot, h] = p.astype(BF16)
            qe_s[slot, h] = qe
            kd_s[slot, h] = (k[:, ks] * jnp.exp2(b_last - b)).astype(BF16)
            dec_s[slot, h] = jnp.exp2(b_last)

    def output_stage(n, slot):
        v = v_ref[0, rows(n), :]
        gg = gg_ref[0, rows(n), :].astype(F32)
        for h, vs in enumerate(vslices):
            st = state_ref[h]
            vh = v[:, vs]
            o = (lax.dot_general(qe_s[slot, h], st.astype(BF16), nt, preferred_element_type=F32)
                 + jnp.dot(p_s[slot, h], vh, preferred_element_type=F32))
            upd = lax.dot_general(vh, kd_s[slot, h], (((0,), (0,)), ((), ())),
                                  preferred_element_type=F32)
            state_ref[h] = st * dec_s[slot, h] + upd
            on = o * lax.rsqrt(jnp.mean(o * o, axis=-1, keepdims=True) + LN_EPS)
            g = gg[:, vs]
            on = on * gn_ref[:, vs] * (g * _sigmoid(g))
            o_ref[0, rows(n), vs] = on.astype(BF16)

    def run(fast):
        decay_stage(0, 0, fast)
        decay_stage(1, 1, fast)
        score_stage(0, 0, fast)

        def pair(p, carry):
            n = 2 * p
            decay_stage(n + 2, 0, fast)
            score_stage(n + 1, 1, fast)
            output_stage(n, 0)
            decay_stage(n + 3, 1, fast)
            score_stage(n + 2, 0, fast)
            output_stage(n + 1, 1)
            return carry

        assert n_chunks % 2 == 0
        lax.fori_loop(0, (n_chunks - 2) // 2, pair, 0)
        score_stage(n_chunks - 1, 1, fast)
        output_stage(n_chunks - 2, 0)
        output_stage(n_chunks - 1, 1)

    safe = jnp.min(xa_ref[0] + ba2) >= GLA_FAST_MIN_X2

    @pl.when(safe)
    def _():
        run(True)

    @pl.when(jnp.logical_not(safe))
    def _():
        run(False)


def _gla(hb3, xa3, b_alpha, gla_norm_g):
    B, S, _ = hb3.shape
    ts = 2048
    kern = functools.partial(_gla_kernel, ts=ts)
    n_masks = 2 + len(GLA_LEVELS)
    H, C, L = GLA_HEADS, GLA_CHUNK, len(GLA_LEVELS)
    return pl.pallas_call(
        kern,
        grid=(B, S // ts),
        in_specs=[pl.BlockSpec((1, ts, GLA_QK), lambda b, s: (b, s, 2 * HB_GQK)),
                  pl.BlockSpec((1, ts, GLA_QK), lambda b, s: (b, s, 2 * HB_GQK + 1)),
                  pl.BlockSpec((1, ts, GLA_QK), lambda b, s: (b, s, 0)),
                  pl.BlockSpec((1, ts, GLA_VW), lambda b, s: (b, s, HB_GV)),
                  pl.BlockSpec((1, ts, GLA_VW), lambda b, s: (b, s, HB_GG)),
                  pl.BlockSpec((1, GLA_QK), lambda b, s: (0, 0)),
                  pl.BlockSpec((1, GLA_VW), lambda b, s: (0, 0)),
                  pl.BlockSpec((n_masks, C, C), lambda b, s: (0, 0, 0)),
                  pl.BlockSpec(((1 + L) * C, 3 * C), lambda b, s: (0, 0))],
        out_specs=pl.BlockSpec((1, ts, GLA_VW), lambda b, s: (b, s, 0)),
        out_shape=jax.ShapeDtypeStruct((B, S, GLA_VW), BF16),
        scratch_shapes=[pltpu.VMEM((H, GLA_DV, GLA_DK), F32),
                        pltpu.VMEM((2, H, C, GLA_DK), F32),
                        pltpu.VMEM((2, H, L, C, GLA_DK), BF16),
                        pltpu.VMEM((2, H, L, C, GLA_DK), BF16),
                        pltpu.VMEM((2, H, C, C), BF16),
                        pltpu.VMEM((2, H, C, GLA_DK), BF16),
                        pltpu.VMEM((2, H, C, GLA_DK), BF16),
                        pltpu.VMEM((2, H, 1, GLA_DK), F32)],
        compiler_params=pltpu.CompilerParams(
            dimension_semantics=("arbitrary", "arbitrary"), vmem_limit_bytes=VMEM_LIMIT),
        name="gla",
    )(hb3, hb3, xa3, hb3, hb3, b_alpha.reshape(1, GLA_QK), gla_norm_g.reshape(1, GLA_VW),
      jnp.asarray(_gla_pair_masks()), jnp.asarray(_gla_cum_matrix(), dtype=BF16))


SB_TQ = 128
SB_SUB = 128
SB_NSUB = 3
SB_HP = 4
SB_MASKED = -1e30
SB_DEAD2 = -152.0


def _sb_suffix_matrix():
    j = np.arange(2 * SB_SUB)[:, None] % SB_SUB
    s = np.arange(2 * SB_SUB)[None, :]
    return ((s >= SB_SUB) | (j > s)).astype(np.float32)


def _sb_window_bias():
    r = np.arange(SB_TQ)[:, None]
    c = np.arange(SB_NSUB * SB_SUB)[None, :]
    return np.stack([np.where(c < r + d * SB_SUB, 0.0, SB_MASKED)
                     for d in range(SB_NSUB)]).astype(np.float32)


def _sb_scores(q, k):
    return lax.dot_general(q, k, (((1,), (1,)), ((), ())), preferred_element_type=F32)


def _sb_split(z):
    n_sub = z.shape[1] // SB_SUB
    ls = _log2_sigmoid2(z)
    lk = ls - z
    hi = lk.astype(BF16)
    lo = (lk - hi.astype(F32)).astype(BF16)
    parts = []
    for c in range(n_sub):
        cs = slice(c * SB_SUB, (c + 1) * SB_SUB)
        parts += [hi[:, cs], lo[:, cs]]
    return ls, jnp.concatenate(parts, axis=1)


def _sb_weights(ls, hl, carry, mo):
    n_sub = ls.shape[1] // SB_SUB
    ws = [None] * n_sub
    for c in reversed(range(n_sub)):
        r = jnp.dot(hl[:, 2 * c * SB_SUB:2 * (c + 1) * SB_SUB], mo, preferred_element_type=F32)
        after = r[:, :SB_SUB] if carry is None else r[:, :SB_SUB] + carry
        ws[c] = jnp.exp2(ls[:, c * SB_SUB:(c + 1) * SB_SUB] + after).astype(BF16)
        carry = r[:, SB_SUB:] if carry is None else carry + r[:, SB_SUB:]
    return (jnp.concatenate(ws, axis=1) if n_sub > 1 else ws[0]), carry


def _sb_kernel(q_ref, k_ref, v_ref, mo_ref, bias_ref, o_ref, z_s, ls_s, hl_s, w_s, c_s, call_s,
               live_s):
    T = SB_TQ
    W = SB_NSUB * SB_SUB
    n_q = q_ref.shape[1] // T
    heads = [slice(h * SB_DH, (h + 1) * SB_DH) for h in range(SB_HP)]

    def row0(i):
        return i * T if isinstance(i, int) else pl.multiple_of(i * T, T)

    def window(i):
        if isinstance(i, int):
            return max(i + 1 - SB_NSUB, 0) * T, min(i, SB_NSUB - 1)
        return row0(jnp.maximum(i + 1 - SB_NSUB, 0)), jnp.minimum(i, SB_NSUB - 1)

    def scores_stage(i, slot):
        q0 = row0(i)
        k0, d = window(i)
        for h, hs in enumerate(heads):
            z = _sb_scores(q_ref[0, pl.ds(q0, T), hs], k_ref[0, pl.ds(k0, W), hs])
            if isinstance(i, int) and i < SB_NSUB - 1:
                z_s[slot, h] = z + bias_ref[d]
            else:
                z_s[slot, h] = jnp.concatenate(
                    [z[:, :W - SB_SUB],
                     z[:, W - SB_SUB:] + bias_ref[SB_NSUB - 1, :, W - SB_SUB:]], axis=1)

    def split_stage(slot):
        for h in range(SB_HP):
            ls, hl = _sb_split(z_s[slot, h])
            ls_s[slot, h] = ls
            hl_s[slot, h] = hl

    def weights_stage(i, slot):
        for h in range(SB_HP):
            w, c = _sb_weights(ls_s[slot, h], hl_s[slot, h], None, mo_ref[...])
            w_s[slot, h] = w
            c_s[slot, h] = c
            call_s[i, h] = c

    def output_stage(i, slot):
        q0 = row0(i)
        k0, _ = window(i)
        for h, hs in enumerate(heads):
            acc = jnp.dot(w_s[slot, h], v_ref[0, pl.ds(k0, W), hs], preferred_element_type=F32)
            o_ref[0, pl.ds(q0, T), hs] = acc.astype(BF16)
        live = functools.reduce(jnp.maximum, [c_s[slot, h] for h in range(SB_HP)])
        live_s[i] = jnp.max(live)

    def steady(j, s):
        scores_stage(j, s)
        split_stage(1 - s)
        weights_stage(j - 2, s)
        output_stage(j - 3, 1 - s)

    scores_stage(0, 0)
    scores_stage(1, 1)
    split_stage(0)
    scores_stage(2, 0)
    split_stage(1)
    weights_stage(0, 0)

    def pair(p, carry):
        j = 3 + 2 * p
        steady(j, 1)
        steady(j + 1, 0)
        return carry

    n_pairs = (n_q - 3) // 2
    lax.fori_loop(0, n_pairs, pair, 0)
    for j in range(3 + 2 * n_pairs, n_q):
        steady(j, j % 2)
    last = (n_q - 1) % 2
    split_stage(last)
    weights_stage(n_q - 2, 1 - last)
    output_stage(n_q - 3, last)
    weights_stage(n_q - 1, last)
    output_stage(n_q - 2, 1 - last)
    output_stage(n_q - 1, last)

    def continue_tile(i):
        q0 = row0(i)
        k0, _ = window(i)
        qs = [q_ref[0, pl.ds(q0, T), hs] for hs in heads]

        def cond(st):
            kk, carries, _ = st
            live = functools.reduce(jnp.maximum, carries)
            return jnp.logical_and(kk >= 0, jnp.max(live) > SB_DEAD2)

        def body(st):
            kk, carries, accs = st
            kk = pl.multiple_of(kk, SB_SUB)
            new_c, new_a = [], []
            for h, hs in enumerate(heads):
                ls, hl = _sb_split(_sb_scores(qs[h], k_ref[0, pl.ds(kk, SB_SUB), hs]))
                w, c = _sb_weights(ls, hl, carries[h], mo_ref[...])
                new_c.append(c)
                new_a.append(accs[h] + jnp.dot(w, v_ref[0, pl.ds(kk, SB_SUB), hs],
                                               preferred_element_type=F32))
            return kk - SB_SUB, tuple(new_c), tuple(new_a)

        carries = tuple(call_s[i, h] for h in range(SB_HP))
        accs = tuple(o_ref[0, pl.ds(q0, T), hs].astype(F32) for hs in heads)
        _, _, accs = lax.while_loop(cond, body, (k0 - SB_SUB, carries, accs))
        for h, hs in enumerate(heads):
            o_ref[0, pl.ds(q0, T), hs] = accs[h].astype(BF16)

    def second_pass(i, carry):
        @pl.when(live_s[i] > SB_DEAD2)
        def _():
            continue_tile(i)
        return carry

    lax.fori_loop(SB_NSUB, n_q, second_pass, 0)


def _stickbreak(hb3):
    B, S, _ = hb3.shape
    wblk = SB_HP * SB_DH
    per_group = D_MODEL // wblk
    blk = lambda group: pl.BlockSpec((1, S, wblk), lambda b, g: (b, 0, group * per_group + g))
    const = lambda a: pl.BlockSpec(a.shape, lambda b, g: (0, 0))
    mo = jnp.asarray(_sb_suffix_matrix(), dtype=BF16)
    bias = jnp.asarray(_sb_window_bias())
    win = SB_NSUB * SB_SUB
    return pl.pallas_call(
        _sb_kernel,
        grid=(B, SB_HEADS // SB_HP),
        in_specs=[blk(HB_SQ), blk(HB_SK), blk(HB_SV), const(mo),
                  pl.BlockSpec(bias.shape, lambda b, g: (0, 0, 0))],
        out_specs=pl.BlockSpec((1, S, wblk), lambda b, g: (b, 0, g)),
        out_shape=jax.ShapeDtypeStruct((B, S, SB_HEADS * SB_DH), BF16),
        scratch_shapes=[pltpu.VMEM((2, SB_HP, SB_TQ, win), F32),
                        pltpu.VMEM((2, SB_HP, SB_TQ, win), F32),
                        pltpu.VMEM((2, SB_HP, SB_TQ, 2 * win), BF16),
                        pltpu.VMEM((2, SB_HP, SB_TQ, win), BF16),
                        pltpu.VMEM((2, SB_HP, SB_TQ, SB_SUB), F32),
                        pltpu.VMEM((S // SB_TQ, SB_HP, SB_TQ, SB_SUB), F32),
                        pltpu.SMEM((S // SB_TQ,), F32)],
        compiler_params=pltpu.CompilerParams(
            dimension_semantics=("arbitrary", "arbitrary"), vmem_limit_bytes=VMEM_LIMIT),
        name="stickbreak",
    )(hb3, hb3, hb3, mo, bias)


POST_TM = 512
FFN_BLK = 256
POST_ROWS = 128


def _post_kernel(x_ref, og_ref, os_ref, ma_ref, mb_ref, mod_e_ref, mod_l_ref, wout_ref, wg_ref,
                 wu_ref, wo_ref, g1_ref, b1_ref, g2_ref, b2_ref, o_ref, x1_s, u2_s, a_s):
    t = pl.program_id(0)

    @pl.when(t == 0)
    def _():
        x1_s[...] = jnp.zeros_like(x1_s)
        u2_s[...] = jnp.zeros_like(u2_s)

    n_ff = D_FF // FFN_BLK
    n_rows = x_ref.shape[0] // POST_ROWS

    def early_rows(slot, r):
        rs = slice(r * POST_ROWS, (r + 1) * POST_ROWS)
        merged = (_sigmoid(ma_ref[rs, :].astype(F32)) * og_ref[rs, :].astype(F32)
                  + _sigmoid(mb_ref[rs, :].astype(F32)) * os_ref[rs, :].astype(F32))
        y = jnp.dot(merged.astype(BF16), wout_ref[...], preferred_element_type=F32)
        gate1 = mod_e_ref[0, :, 2 * D_MODEL:3 * D_MODEL]
        shift2 = mod_e_ref[0, :, 3 * D_MODEL:4 * D_MODEL]
        scale2 = mod_e_ref[0, :, 4 * D_MODEL:5 * D_MODEL]
        x1 = _ln(ALPHA * x_ref[rs, :] + (1.0 + gate1) * y) * g1_ref[...] + b1_ref[...]
        x1_s[slot, rs, :] = x1
        u2_s[slot, rs, :] = (_ln(x1) * (1.0 + scale2) + shift2).astype(BF16)

    def ffn_block(slot, f):
        cols = slice(f * FFN_BLK, (f + 1) * FFN_BLK)
        u = u2_s[slot]
        g = jnp.dot(u, wg_ref[:, cols], preferred_element_type=F32)
        up = jnp.dot(u, wu_ref[:, cols], preferred_element_type=F32)
        a_s[:, cols] = (g * _sigmoid(g) * up).astype(BF16)

    def out_rows(slot, r):
        rs = slice(r * POST_ROWS, (r + 1) * POST_ROWS)
        ff = jnp.dot(a_s[rs, :], wo_ref[...], preferred_element_type=F32)
        gate2 = mod_l_ref[0, :, 5 * D_MODEL:6 * D_MODEL]
        o_ref[rs, :] = (_ln(ALPHA * x1_s[slot, rs, :] + (1.0 + gate2) * ff) * g2_ref[...]
                        + b2_ref[...])

    def step(e_slot, l_slot):
        every = (n_ff + n_rows - 1) // n_rows
        for f in range(n_ff):
            ffn_block(l_slot, f)
            if f % every == 0 and f // every < n_rows:
                early_rows(e_slot, f // every)
        for r in range(n_rows):
            out_rows(l_slot, r)

    @pl.when(t % 2 == 0)
    def _():
        step(0, 1)

    @pl.when(t % 2 == 1)
    def _():
        step(1, 0)


def _post(x2d, og2, os2, hb2, mod3, w_out, w_ffn_in, w_ffn_out, ln1_g, ln1_b, ln2_g, ln2_b, seq):
    M, D = x2d.shape
    tm = POST_TM
    n = M // tm
    per_b = seq // tm
    e_tile = lambda t: jnp.minimum(t, n - 1)
    l_tile = lambda t: jnp.maximum(t - 1, 0)
    once = pl.Buffered(1)
    row_e = lambda c: pl.BlockSpec((tm, D), lambda t: (e_tile(t), c))
    vec = pl.BlockSpec((1, D), lambda t: (0, 0), pipeline_mode=once)
    return pl.pallas_call(
        _post_kernel,
        grid=(n + 1,),
        in_specs=[row_e(0), row_e(0), row_e(0), row_e(HB_MA), row_e(HB_MB),
                  pl.BlockSpec((1, 1, 6 * D_MODEL), lambda t: (e_tile(t) // per_b, 0, 0)),
                  pl.BlockSpec((1, 1, 6 * D_MODEL), lambda t: (l_tile(t) // per_b, 0, 0)),
                  pl.BlockSpec((D, D), lambda t: (0, 0), pipeline_mode=once),
                  pl.BlockSpec((D, D_FF), lambda t: (0, 0), pipeline_mode=once),
                  pl.BlockSpec((D, D_FF), lambda t: (0, 1), pipeline_mode=once),
                  pl.BlockSpec((D_FF, D), lambda t: (0, 0), pipeline_mode=once),
                  vec, vec, vec, vec],
        out_specs=pl.BlockSpec((tm, D), lambda t: (l_tile(t), 0)),
        out_shape=jax.ShapeDtypeStruct((M, D), F32),
        scratch_shapes=[pltpu.VMEM((2, tm, D), F32),
                        pltpu.VMEM((2, tm, D), BF16),
                        pltpu.VMEM((tm, D_FF), BF16)],
        compiler_params=pltpu.CompilerParams(
            dimension_semantics=("arbitrary",), vmem_limit_bytes=VMEM_LIMIT),
        name="post",
    )(x2d, og2, os2, hb2, hb2, mod3, mod3, w_out, w_ffn_in, w_ffn_in, w_ffn_out,
      ln1_g.reshape(1, D), ln1_b.reshape(1, D), ln2_g.reshape(1, D), ln2_b.reshape(1, D))


def _regroup_w_in(w_in_l):
    ga = 2 * GLA_QK + 2 * GLA_VW
    w16 = w_in_l.astype(BF16)
    w_all = jnp.concatenate([w16[:, :ga], w16[:, ga + GLA_LOWRANK:]], axis=1)
    return w_all, w_in_l[:, ga:ga + GLA_LOWRANK]


def _colscale():
    ones = lambda n: jnp.ones((n,), F32)
    return jnp.concatenate([
        ones(GLA_QK) * (GLA_DK ** -0.5), ones(GLA_QK), ones(2 * GLA_VW),
        ones(D_MODEL) * (SB_DH ** -0.5 * LOG2E), ones(4 * D_MODEL)]).reshape(1, HB_W)


def kernel(x, c, w_ada, b_ada, w_in, w_alpha_up, b_alpha, gla_norm_g, w_out,
           ln1_g, ln1_b, w_ffn_in, w_ffn_out, ln2_g, ln2_b):
    B, S, D = x.shape
    M = B * S
    x2d = x.reshape(M, D)
    for l in range(DEPTH):
        mod3 = _adaln_mod(c, w_ada[l], b_ada[l]).reshape(B, 1, 6 * D)
        w_all, w_ga = _regroup_w_in(w_in[l])
        w_decay = _decay_weight(w_ga, w_alpha_up[l]).astype(BF16)
        hb, xa = _in_proj(x2d, mod3, w_all, _colscale(), w_decay, S)
        hb3 = hb.reshape(B, S, HB_W)
        og = _gla(hb3, xa.reshape(B, S, GLA_QK), b_alpha[l], gla_norm_g[l])
        osb = _stickbreak(hb3)
        x2d = _post(x2d, og.reshape(M, D), osb.reshape(M, D), hb, mod3, w_out[l].astype(BF16),
                    w_ffn_in[l].astype(BF16), w_ffn_out[l].astype(BF16),
                    ln1_g[l], ln1_b[l], ln2_g[l], ln2_b[l], S)
    return x2d.reshape(B, S, D)
```

```python
import functools
import math

import numpy as np
import jax
import jax.numpy as jnp
from jax import lax
from jax.experimental import pallas as pl
from jax.experimental.pallas import tpu as pltpu

F32 = jnp.float32
BF16 = jnp.bfloat16

D_MODEL = 1024
GLA_HEADS = 4
GLA_DK = 128
GLA_DV = 256
GLA_QK = GLA_HEADS * GLA_DK
GLA_VW = GLA_HEADS * GLA_DV
GLA_LOWRANK = 16
GLA_TAU = 16.0
GLA_CHUNK = 64
SB_HEADS = 8
SB_DH = 128
D_FF = 2816
DEPTH = 1
ALPHA = (2.0 * DEPTH) ** 0.25
LN_EPS = 1e-5
LOG2E = math.log2(math.e)

HB_W = 8 * D_MODEL
IN_TM = 512
IN_TN = 1024
HB_GQK, HB_GV, HB_GG, HB_SQ, HB_SK, HB_SV, HB_MA, HB_MB = 0, 1, 2, 3, 4, 5, 6, 7

VMEM_LIMIT = 56 * 1024 * 1024
LN_ROWS = 128


def _ln(x):
    mu = jnp.mean(x, axis=-1, keepdims=True)
    xc = x - mu
    var = jnp.mean(xc * xc, axis=-1, keepdims=True)
    return xc * lax.rsqrt(var + LN_EPS)


def _sigmoid(x):
    return 1.0 / (1.0 + jnp.exp(-x))


def _log2_sigmoid2(x2):
    return jnp.minimum(x2, 0.0) - jnp.log(1.0 + jnp.exp2(-jnp.abs(x2))) * LOG2E


def _adaln_kernel(c_ref, w_ref, b_ref, o_ref):
    c = c_ref[...]
    ca = c * _sigmoid(c)
    o_ref[...] = jnp.dot(ca, w_ref[...], preferred_element_type=F32,
                         precision=lax.Precision.HIGHEST) + b_ref[...]


def _adaln_mod(c, w_ada, b_ada):
    B, D = c.shape
    N = w_ada.shape[1]
    tn = 1024
    return pl.pallas_call(
        _adaln_kernel,
        grid=(N // tn,),
        in_specs=[pl.BlockSpec((B, D), lambda j: (0, 0)),
                  pl.BlockSpec((D, tn), lambda j: (0, j)),
                  pl.BlockSpec((1, tn), lambda j: (0, j))],
        out_specs=pl.BlockSpec((B, tn), lambda j: (0, j)),
        out_shape=jax.ShapeDtypeStruct((B, N), F32),
        name="adaln_mod",
    )(c, w_ada, b_ada.reshape(1, N))


def _decay_weight_kernel(wga_ref, wup_ref, o_ref):
    o_ref[...] = lax.dot_general(wga_ref[...], wup_ref[...], (((0,), (0,)), ((), ())),
                                 preferred_element_type=F32, precision=lax.Precision.HIGHEST)


def _decay_weight(w_ga, w_up):
    D = w_ga.shape[1]
    return pl.pallas_call(
        _decay_weight_kernel,
        out_shape=jax.ShapeDtypeStruct((D, GLA_QK), F32),
        name="decay_weight",
    )(w_ga, w_up)


def _inproj_kernel(x_ref, mod_ref, w_ref, cs_ref, wa_ref, hb_ref, xa_ref, u_s):
    t = pl.program_id(0)
    n_col = HB_W // IN_TN
    n_rows = x_ref.shape[0] // LN_ROWS

    @pl.when(t == 0)
    def _():
        u_s[...] = jnp.zeros_like(u_s)

    def ln_rows(slot, r):
        rs = slice(r * LN_ROWS, (r + 1) * LN_ROWS)
        shift = mod_ref[0, :, 0 * D_MODEL:1 * D_MODEL]
        scale1p = 1.0 + mod_ref[0, :, 1 * D_MODEL:2 * D_MODEL]
        u_s[slot, rs, :] = (_ln(x_ref[rs, :]) * scale1p + shift).astype(BF16)

    def step(ln_slot, mm_slot):
        u = u_s[mm_slot]
        for c in range(n_col):
            cols = slice(c * IN_TN, (c + 1) * IN_TN)
            acc = lax.dot_general(u, w_ref[cols, :], (((1,), (1,)), ((), ())),
                                  preferred_element_type=F32) * cs_ref[:, cols]
            hb_ref[:, cols] = acc.astype(BF16)
            if c < n_rows:
                ln_rows(ln_slot, c)
        xa_ref[...] = jnp.dot(u, wa_ref[...], preferred_element_type=F32) * LOG2E

    @pl.when(t % 2 == 0)
    def _():
        step(0, 1)

    @pl.when(t % 2 == 1)
    def _():
        step(1, 0)


def _in_proj(x2d, mod3, w_all, colscale, w_decay, seq):
    M, D = x2d.shape
    tm = IN_TM
    n = M // tm
    per_b = seq // tm
    assert tm // LN_ROWS <= HB_W // IN_TN
    ln_tile = lambda t: jnp.minimum(t, n - 1)
    mm_tile = lambda t: jnp.maximum(t - 1, 0)
    once = pl.Buffered(1)
    return pl.pallas_call(
        _inproj_kernel,
        grid=(n + 1,),
        in_specs=[pl.BlockSpec((tm, D), lambda t: (ln_tile(t), 0)),
                  pl.BlockSpec((1, 1, 6 * D_MODEL), lambda t: (ln_tile(t) // per_b, 0, 0)),
                  pl.BlockSpec((HB_W, D), lambda t: (0, 0), pipeline_mode=once),
                  pl.BlockSpec((1, HB_W), lambda t: (0, 0), pipeline_mode=once),
                  pl.BlockSpec((D, GLA_QK), lambda t: (0, 0), pipeline_mode=once)],
        out_specs=[pl.BlockSpec((tm, HB_W), lambda t: (mm_tile(t), 0)),
                   pl.BlockSpec((tm, GLA_QK), lambda t: (mm_tile(t), 0))],
        out_shape=[jax.ShapeDtypeStruct((M, HB_W), BF16),
                   jax.ShapeDtypeStruct((M, GLA_QK), F32)],
        scratch_shapes=[pltpu.VMEM((2, tm, D), BF16)],
        compiler_params=pltpu.CompilerParams(
            dimension_semantics=("arbitrary",), vmem_limit_bytes=VMEM_LIMIT),
        name="in_proj",
    )(x2d, mod3, w_all, colscale, w_decay)


GLA_LEVELS = (32, 16, 8, 4, 2, 1)
GLA_FAST_MAX_DECAY = 60.0
GLA_FAST_MIN_X2 = 1.0 - GLA_FAST_MAX_DECAY * GLA_TAU / GLA_CHUNK


def _gla_cum_matrix():
    C = GLA_CHUNK
    r = np.arange(C)[:, None]
    s = np.arange(C)[None, :]
    blocks = [s <= r]
    for blk in GLA_LEVELS:
        blocks.append(s <= (r // (2 * blk)) * (2 * blk) + blk - 1)
    one = np.concatenate(blocks, axis=0).astype(np.float32)
    return np.concatenate([one, one, one], axis=1)


def _gla_pair_masks():
    C = GLA_CHUNK
    r = np.arange(C)[:, None]
    s = np.arange(C)[None, :]
    masks = [r == s]
    for blk in GLA_LEVELS:
        masks.append(((r // blk) == (s // blk) + 1) & ((r // (2 * blk)) == (s // (2 * blk))))
    masks.append(s <= r)
    return np.stack(masks).astype(np.float32)


def _gla_kernel(q_ref, k_ref, xa_ref, v_ref, gg_ref, ba_ref, gn_ref, pm_ref, cum_ref, o_ref,
                state_ref, b_s, ql_s, kl_s, p_s, qe_s, kd_s, dec_s, *, ts):
    C = GLA_CHUNK
    n_levels = len(GLA_LEVELS)
    n_chunks = ts // C
    nt = (((1,), (1,)), ((), ()))
    kslices = [slice(h * GLA_DK, (h + 1) * GLA_DK) for h in range(GLA_HEADS)]
    vslices = [slice(h * GLA_DV, (h + 1) * GLA_DV) for h in range(GLA_HEADS)]

    @pl.when(pl.program_id(1) == 0)
    def _():
        state_ref[...] = jnp.zeros_like(state_ref)

    ba2 = ba_ref[...] * LOG2E

    def rows(n):
        return pl.ds(n * C if isinstance(n, int) else pl.multiple_of(n * C, C), C)

    def decay_stage(n, slot, fast):
        la = _log2_sigmoid2(xa_ref[0, rows(n), :] + ba2) * (1.0 / GLA_TAU)
        la_hi = la.astype(BF16)
        rem = la - la_hi.astype(F32)
        la_mid = rem.astype(BF16)
        la_lo = (rem - la_mid.astype(F32)).astype(BF16)
        la3 = jnp.concatenate([la_hi, la_mid, la_lo], axis=0)
        if fast:
            ball = jnp.dot(cum_ref[0:C, :], la3, preferred_element_type=F32)
            for h, ks in enumerate(kslices):
                b_s[slot, h] = ball[:, ks]
            return
        ball = jnp.dot(cum_ref[...], la3, preferred_element_type=F32)
        q = q_ref[0, rows(n), :].astype(F32)
        k = k_ref[0, rows(n), :].astype(F32)
        for h, ks in enumerate(kslices):
            b = ball[0:C, ks]
            b_s[slot, h] = b
            for li in range(n_levels):
                f = jnp.exp2(-jnp.abs(b - ball[(li + 1) * C:(li + 2) * C, ks]))
                ql_s[slot, h, li] = (q[:, ks] * f).astype(BF16)
                kl_s[slot, h, li] = (k[:, ks] * f).astype(BF16)

    def score_stage(n, slot, fast):
        q16 = q_ref[0, rows(n), :]
        k16 = k_ref[0, rows(n), :]
        q = q16.astype(F32)
        k = k16.astype(F32)
        for h, ks in enumerate(kslices):
            b = b_s[slot, h]
            b_last = b[C - 1:C, :]
            qe = (q[:, ks] * jnp.exp2(b)).astype(BF16)
            if fast:
                ki = (k[:, ks] * jnp.exp2(-b)).astype(BF16)
                sc = lax.dot_general(qe, ki, nt, preferred_element_type=F32)
                p = jnp.where(pm_ref[n_levels + 1] != 0.0, sc, 0.0)
            else:
                sc = lax.dot_general(q16[:, ks], k16[:, ks], nt, preferred_element_type=F32)
                p = jnp.where(pm_ref[0] != 0.0, sc, 0.0)
                for li in range(n_levels):
                    sc = lax.dot_general(ql_s[slot, h, li], kl_s[slot, h, li], nt,
                                         preferred_element_type=F32)
                    p = jnp.where(pm_ref[li + 1] != 0.0, sc, p)
            p_s[slot, h] = p.astype(BF16)
            qe_s[slot, h] = qe
            kd_s[slot, h] = (k[:, ks] * jnp.exp2(b_last - b)).astype(BF16)
            dec_s[slot, h] = jnp.exp2(b_last)

    def output_stage(n, slot):
        v = v_ref[0, rows(n), :]
        gg = gg_ref[0, rows(n), :].astype(F32)
        for h, vs in enumerate(vslices):
            st = state_ref[h]
            vh = v[:, vs]
            o = (lax.dot_general(qe_s[slot, h], st.astype(BF16), nt, preferred_element_type=F32)
                 + jnp.dot(p_s[slot, h], vh, preferred_element_type=F32))
            upd = lax.dot_general(vh, kd_s[slot, h], (((0,), (0,)), ((), ())),
                                  preferred_element_type=F32)
            state_ref[h] = st * dec_s[slot, h] + upd
            on = o * lax.rsqrt(jnp.mean(o * o, axis=-1, keepdims=True) + LN_EPS)
            g = gg[:, vs]
            on = on * gn_ref[:, vs] * (g * _sigmoid(g))
            o_ref[0, rows(n), vs] = on.astype(BF16)

    def run(fast):
        decay_stage(0, 0, fast)
        decay_stage(1, 1, fast)
        score_stage(0, 0, fast)

        def pair(p, carry):
            n = 2 * p
            decay_stage(n + 2, 0, fast)
            score_stage(n + 1, 1, fast)
            output_stage(n, 0)
            decay_stage(n + 3, 1, fast)
            score_stage(n + 2, 0, fast)
            output_stage(n + 1, 1)
            return carry

        assert n_chunks % 2 == 0
        lax.fori_loop(0, (n_chunks - 2) // 2, pair, 0)
        score_stage(n_chunks - 1, 1, fast)
        output_stage(n_chunks - 2, 0)
        output_stage(n_chunks - 1, 1)

    safe = jnp.min(xa_ref[0] + ba2) >= GLA_FAST_MIN_X2

    @pl.when(safe)
    def _():
        run(True)

    @pl.when(jnp.logical_not(safe))
    def _():
        run(False)


def _gla(hb3, xa3, b_alpha, gla_norm_g):
    B, S, _ = hb3.shape
    ts = 2048
    kern = functools.partial(_gla_kernel, ts=ts)
    n_masks = 2 + len(GLA_LEVELS)
    H, C, L = GLA_HEADS, GLA_CHUNK, len(GLA_LEVELS)
    return pl.pallas_call(
        kern,
        grid=(B, S // ts),
        in_specs=[pl.BlockSpec((1, ts, GLA_QK), lambda b, s: (b, s, 2 * HB_GQK)),
                  pl.BlockSpec((1, ts, GLA_QK), lambda b, s: (b, s, 2 * HB_GQK + 1)),
                  pl.BlockSpec((1, ts, GLA_QK), lambda b, s: (b, s, 0)),
                  pl.BlockSpec((1, ts, GLA_VW), lambda b, s: (b, s, HB_GV)),
                  pl.BlockSpec((1, ts, GLA_VW), lambda b, s: (b, s, HB_GG)),
                  pl.BlockSpec((1, GLA_QK), lambda b, s: (0, 0)),
                  pl.BlockSpec((1, GLA_VW), lambda b, s: (0, 0)),
                  pl.BlockSpec((n_masks, C, C), lambda b, s: (0, 0, 0)),
                  pl.BlockSpec(((1 + L) * C, 3 * C), lambda b, s: (0, 0))],
        out_specs=pl.BlockSpec((1, ts, GLA_VW), lambda b, s: (b, s, 0)),
        out_shape=jax.ShapeDtypeStruct((B, S, GLA_VW), BF16),
        scratch_shapes=[pltpu.VMEM((H, GLA_DV, GLA_DK), F32),
                        pltpu.VMEM((2, H, C, GLA_DK), F32),
                        pltpu.VMEM((2, H, L, C, GLA_DK), BF16),
                        pltpu.VMEM((2, H, L, C, GLA_DK), BF16),
                        pltpu.VMEM((2, H, C, C), BF16),
                        pltpu.VMEM((2, H, C, GLA_DK), BF16),
                        pltpu.VMEM((2, H, C, GLA_DK), BF16),
                        pltpu.VMEM((2, H, 1, GLA_DK), F32)],
        compiler_params=pltpu.CompilerParams(
            dimension_semantics=("arbitrary", "arbitrary"), vmem_limit_bytes=VMEM_LIMIT),
        name="gla",
    )(hb3, hb3, xa3, hb3, hb3, b_alpha.reshape(1, GLA_QK), gla_norm_g.reshape(1, GLA_VW),
      jnp.asarray(_gla_pair_masks()), jnp.asarray(_gla_cum_matrix(), dtype=BF16))


SB_TQ = 128
SB_SUB = 128
SB_NSUB = 3
SB_HP = 4
SB_MASKED = -1e30
SB_DEAD2 = -152.0


def _sb_suffix_matrix():
    j = np.arange(2 * SB_SUB)[:, None] % SB_SUB
    s = np.arange(2 * SB_SUB)[None, :]
    return ((s >= SB_SUB) | (j > s)).astype(np.float32)


def _sb_window_bias():
    r = np.arange(SB_TQ)[:, None]
    c = np.arange(SB_NSUB * SB_SUB)[None, :]
    return np.stack([np.where(c < r + d * SB_SUB, 0.0, SB_MASKED)
                     for d in range(SB_NSUB)]).astype(np.float32)


def _sb_scores(q, k):
    return lax.dot_general(q, k, (((1,), (1,)), ((), ())), preferred_element_type=F32)


def _sb_split(z):
    n_sub = z.shape[1] // SB_SUB
    ls = _log2_sigmoid2(z)
    lk = ls - z
    hi = lk.astype(BF16)
    lo = (lk - hi.astype(F32)).astype(BF16)
    parts = []
    for c in range(n_sub):
        cs = slice(c * SB_SUB, (c + 1) * SB_SUB)
        parts += [hi[:, cs], lo[:, cs]]
    return ls, jnp.concatenate(parts, axis=1)


def _sb_weights(ls, hl, carry, mo):
    n_sub = ls.shape[1] // SB_SUB
    ws = [None] * n_sub
    for c in reversed(range(n_sub)):
        r = jnp.dot(hl[:, 2 * c * SB_SUB:2 * (c + 1) * SB_SUB], mo, preferred_element_type=F32)
        after = r[:, :SB_SUB] if carry is None else r[:, :SB_SUB] + carry
        ws[c] = jnp.exp2(ls[:, c * SB_SUB:(c + 1) * SB_SUB] + after).astype(BF16)
        carry = r[:, SB_SUB:] if carry is None else carry + r[:, SB_SUB:]
    return (jnp.concatenate(ws, axis=1) if n_sub > 1 else ws[0]), carry


def _sb_kernel(q_ref, k_ref, v_ref, mo_ref, bias_ref, o_ref, z_s, ls_s, hl_s, w_s, c_s, call_s,
               live_s):
    T = SB_TQ
    W = SB_NSUB * SB_SUB
    n_q = q_ref.shape[1] // T
    heads = [slice(h * SB_DH, (h + 1) * SB_DH) for h in range(SB_HP)]

    def row0(i):
        return i * T if isinstance(i, int) else pl.multiple_of(i * T, T)

    def window(i):
        if isinstance(i, int):
            return max(i + 1 - SB_NSUB, 0) * T, min(i, SB_NSUB - 1)
        return row0(jnp.maximum(i + 1 - SB_NSUB, 0)), jnp.minimum(i, SB_NSUB - 1)

    def scores_stage(i, slot):
        q0 = row0(i)
        k0, d = window(i)
        for h, hs in enumerate(heads):
            z = _sb_scores(q_ref[0, pl.ds(q0, T), hs], k_ref[0, pl.ds(k0, W), hs])
            if isinstance(i, int) and i < SB_NSUB - 1:
                z_s[slot, h] = z + bias_ref[d]
            else:
                z_s[slot, h] = jnp.concatenate(
                    [z[:, :W - SB_SUB],
                     z[:, W - SB_SUB:] + bias_ref[SB_NSUB - 1, :, W - SB_SUB:]], axis=1)

    def split_stage(slot):
        for h in range(SB_HP):
            ls, hl = _sb_split(z_s[slot, h])
            ls_s[slot, h] = ls
            hl_s[slot, h] = hl

    def weights_stage(i, slot):
        for h in range(SB_HP):
            w, c = _sb_weights(ls_s[slot, h], hl_s[slot, h], None, mo_ref[...])
            w_s[slot, h] = w
            c_s[slot, h] = c
            call_s[i, h] = c

    def output_stage(i, slot):
        q0 = row0(i)
        k0, _ = window(i)
        for h, hs in enumerate(heads):
            acc = jnp.dot(w_s[slot, h], v_ref[0, pl.ds(k0, W), hs], preferred_element_type=F32)
            o_ref[0, pl.ds(q0, T), hs] = acc.astype(BF16)
        live = functools.reduce(jnp.maximum, [c_s[slot, h] for h in range(SB_HP)])
        live_s[i] = jnp.max(live)

    def steady(j, s):
        scores_stage(j, s)
        split_stage(1 - s)
        weights_stage(j - 2, s)
        output_stage(j - 3, 1 - s)

    scores_stage(0, 0)
    scores_stage(1, 1)
    split_stage(0)
    scores_stage(2, 0)
    split_stage(1)
    weights_stage(0, 0)

    def pair(p, carry):
        j = 3 + 2 * p
        steady(j, 1)
        steady(j + 1, 0)
        return carry

    n_pairs = (n_q - 3) // 2
    lax.fori_loop(0, n_pairs, pair, 0)
    for j in range(3 + 2 * n_pairs, n_q):
        steady(j, j % 2)
    last = (n_q - 1) % 2
    split_stage(last)
    weights_stage(n_q - 2, 1 - last)
    output_stage(n_q - 3, last)
    weights_stage(n_q - 1, last)
    output_stage(n_q - 2, 1 - last)
    output_stage(n_q - 1, last)

    def continue_tile(i):
        q0 = row0(i)
        k0, _ = window(i)
        qs = [q_ref[0, pl.ds(q0, T), hs] for hs in heads]

        def cond(st):
            kk, carries, _ = st
            live = functools.reduce(jnp.maximum, carries)
            return jnp.logical_and(kk >= 0, jnp.max(live) > SB_DEAD2)

        def body(st):
            kk, carries, accs = st
            kk = pl.multiple_of(kk, SB_SUB)
            new_c, new_a = [], []
            for h, hs in enumerate(heads):
                ls, hl = _sb_split(_sb_scores(qs[h], k_ref[0, pl.ds(kk, SB_SUB), hs]))
                w, c = _sb_weights(ls, hl, carries[h], mo_ref[...])
                new_c.append(c)
                new_a.append(accs[h] + jnp.dot(w, v_ref[0, pl.ds(kk, SB_SUB), hs],
                                               preferred_element_type=F32))
            return kk - SB_SUB, tuple(new_c), tuple(new_a)

        carries = tuple(call_s[i, h] for h in range(SB_HP))
        accs = tuple(o_ref[0, pl.ds(q0, T), hs].astype(F32) for hs in heads)
        _, _, accs = lax.while_loop(cond, body, (k0 - SB_SUB, carries, accs))
        for h, hs in enumerate(heads):
            o_ref[0, pl.ds(q0, T), hs] = accs[h].astype(BF16)

    def second_pass(i, carry):
        @pl.when(live_s[i] > SB_DEAD2)
        def _():
            continue_tile(i)
        return carry

    lax.fori_loop(SB_NSUB, n_q, second_pass, 0)


def _stickbreak(hb3):
    B, S, _ = hb3.shape
    wblk = SB_HP * SB_DH
    per_group = D_MODEL // wblk
    blk = lambda group: pl.BlockSpec((1, S, wblk), lambda b, g: (b, 0, group * per_group + g))
    const = lambda a: pl.BlockSpec(a.shape, lambda b, g: (0, 0))
    mo = jnp.asarray(_sb_suffix_matrix(), dtype=BF16)
    bias = jnp.asarray(_sb_window_bias())
    win = SB_NSUB * SB_SUB
    return pl.pallas_call(
        _sb_kernel,
        grid=(B, SB_HEADS // SB_HP),
        in_specs=[blk(HB_SQ), blk(HB_SK), blk(HB_SV), const(mo),
                  pl.BlockSpec(bias.shape, lambda b, g: (0, 0, 0))],
        out_specs=pl.BlockSpec((1, S, wblk), lambda b, g: (b, 0, g)),
        out_shape=jax.ShapeDtypeStruct((B, S, SB_HEADS * SB_DH), BF16),
        scratch_shapes=[pltpu.VMEM((2, SB_HP, SB_TQ, win), F32),
                        pltpu.VMEM((2, SB_HP, SB_TQ, win), F32),
                        pltpu.VMEM((2, SB_HP, SB_TQ, 2 * win), BF16),
                        pltpu.VMEM((2, SB_HP, SB_TQ, win), BF16),
                        pltpu.VMEM((2, SB_HP, SB_TQ, SB_SUB), F32),
                        pltpu.VMEM((S // SB_TQ, SB_HP, SB_TQ, SB_SUB), F32),
                        pltpu.SMEM((S // SB_TQ,), F32)],
        compiler_params=pltpu.CompilerParams(
            dimension_semantics=("arbitrary", "arbitrary"), vmem_limit_bytes=VMEM_LIMIT),
        name="stickbreak",
    )(hb3, hb3, hb3, mo, bias)


POST_TM = 512
FFN_BLK = 256
POST_ROWS = 128


def _post_kernel(x_ref, og_ref, os_ref, ma_ref, mb_ref, mod_e_ref, mod_l_ref, wout_ref, wg_ref,
                 wu_ref, wo_ref, g1_ref, b1_ref, g2_ref, b2_ref, o_ref, x1_s, u2_s, a_s):
    t = pl.program_id(0)

    @pl.when(t == 0)
    def _():
        x1_s[...] = jnp.zeros_like(x1_s)
        u2_s[...] = jnp.zeros_like(u2_s)

    n_ff = D_FF // FFN_BLK
    n_rows = x_ref.shape[0] // POST_ROWS

    def early_rows(slot, r):
        rs = slice(r * POST_ROWS, (r + 1) * POST_ROWS)
        merged = (_sigmoid(ma_ref[rs, :].astype(F32)) * og_ref[rs, :].astype(F32)
                  + _sigmoid(mb_ref[rs, :].astype(F32)) * os_ref[rs, :].astype(F32))
        y = jnp.dot(merged.astype(BF16), wout_ref[...], preferred_element_type=F32)
        gate1 = mod_e_ref[0, :, 2 * D_MODEL:3 * D_MODEL]
        shift2 = mod_e_ref[0, :, 3 * D_MODEL:4 * D_MODEL]
        scale2 = mod_e_ref[0, :, 4 * D_MODEL:5 * D_MODEL]
        x1 = _ln(ALPHA * x_ref[rs, :] + (1.0 + gate1) * y) * g1_ref[...] + b1_ref[...]
        x1_s[slot, rs, :] = x1
        u2_s[slot, rs, :] = (_ln(x1) * (1.0 + scale2) + shift2).astype(BF16)

    def ffn_block(slot, f):
        cols = slice(f * FFN_BLK, (f + 1) * FFN_BLK)
        u = u2_s[slot]
        g = jnp.dot(u, wg_ref[:, cols], preferred_element_type=F32)
        up = jnp.dot(u, wu_ref[:, cols], preferred_element_type=F32)
        a_s[:, cols] = (g * _sigmoid(g) * up).astype(BF16)

    def out_rows(slot, r):
        rs = slice(r * POST_ROWS, (r + 1) * POST_ROWS)
        ff = jnp.dot(a_s[rs, :], wo_ref[...], preferred_element_type=F32)
        gate2 = mod_l_ref[0, :, 5 * D_MODEL:6 * D_MODEL]
        o_ref[rs, :] = (_ln(ALPHA * x1_s[slot, rs, :] + (1.0 + gate2) * ff) * g2_ref[...]
                        + b2_ref[...])

    def step(e_slot, l_slot):
        every = (n_ff + n_rows - 1) // n_rows
        for f in range(n_ff):
            ffn_block(l_slot, f)
            if f % every == 0 and f // every < n_rows:
                early_rows(e_slot, f // every)
        for r in range(n_rows):
            out_rows(l_slot, r)

    @pl.when(t % 2 == 0)
    def _():
        step(0, 1)

    @pl.when(t % 2 == 1)
    def _():
        step(1, 0)


def _post(x2d, og2, os2, hb2, mod3, w_out, w_ffn_in, w_ffn_out, ln1_g, ln1_b, ln2_g, ln2_b, seq):
    M, D = x2d.shape
    tm = POST_TM
    n = M // tm
    per_b = seq // tm
    e_tile = lambda t: jnp.minimum(t, n - 1)
    l_tile = lambda t: jnp.maximum(t - 1, 0)
    once = pl.Buffered(1)
    row_e = lambda c: pl.BlockSpec((tm, D), lambda t: (e_tile(t), c))
    vec = pl.BlockSpec((1, D), lambda t: (0, 0), pipeline_mode=once)
    return pl.pallas_call(
        _post_kernel,
        grid=(n + 1,),
        in_specs=[row_e(0), row_e(0), row_e(0), row_e(HB_MA), row_e(HB_MB),
                  pl.BlockSpec((1, 1, 6 * D_MODEL), lambda t: (e_tile(t) // per_b, 0, 0)),
                  pl.BlockSpec((1, 1, 6 * D_MODEL), lambda t: (l_tile(t) // per_b, 0, 0)),
                  pl.BlockSpec((D, D), lambda t: (0, 0), pipeline_mode=once),
                  pl.BlockSpec((D, D_FF), lambda t: (0, 0), pipeline_mode=once),
                  pl.BlockSpec((D, D_FF), lambda t: (0, 1), pipeline_mode=once),
                  pl.BlockSpec((D_FF, D), lambda t: (0, 0), pipeline_mode=once),
                  vec, vec, vec, vec],
        out_specs=pl.BlockSpec((tm, D), lambda t: (l_tile(t), 0)),
        out_shape=jax.ShapeDtypeStruct((M, D), F32),
        scratch_shapes=[pltpu.VMEM((2, tm, D), F32),
                        pltpu.VMEM((2, tm, D), BF16),
                        pltpu.VMEM((tm, D_FF), BF16)],
        compiler_params=pltpu.CompilerParams(
            dimension_semantics=("arbitrary",), vmem_limit_bytes=VMEM_LIMIT),
        name="post",
    )(x2d, og2, os2, hb2, hb2, mod3, mod3, w_out, w_ffn_in, w_ffn_in, w_ffn_out,
      ln1_g.reshape(1, D), ln1_b.reshape(1, D), ln2_g.reshape(1, D), ln2_b.reshape(1, D))


def _regroup_w_in(w_in_l):
    ga = 2 * GLA_QK + 2 * GLA_VW
    w_t = w_in_l.T
    w16 = w_t.astype(BF16)
    w_all_t = jnp.concatenate([w16[:ga], w16[ga + GLA_LOWRANK:]], axis=0)
    return w_all_t, w_t[ga:ga + GLA_LOWRANK]


def _colscale():
    ones = lambda n: jnp.ones((n,), F32)
    return jnp.concatenate([
        ones(GLA_QK) * (GLA_DK ** -0.5), ones(GLA_QK), ones(2 * GLA_VW),
        ones(D_MODEL) * (SB_DH ** -0.5 * LOG2E), ones(4 * D_MODEL)]).reshape(1, HB_W)


def kernel(x, c, w_ada, b_ada, w_in, w_alpha_up, b_alpha, gla_norm_g, w_out,
           ln1_g, ln1_b, w_ffn_in, w_ffn_out, ln2_g, ln2_b):
    B, S, D = x.shape
    M = B * S
    x2d = x.reshape(M, D)
    for l in range(DEPTH):
        mod3 = _adaln_mod(c, w_ada[l], b_ada[l]).reshape(B, 1, 6 * D)
        w_all, w_ga = _regroup_w_in(w_in[l])
        w_decay = _decay_weight(w_ga, w_alpha_up[l]).astype(BF16)
        hb, xa = _in_proj(x2d, mod3, w_all, _colscale(), w_decay, S)
        hb3 = hb.reshape(B, S, HB_W)
        og = _gla(hb3, xa.reshape(B, S, GLA_QK), b_alpha[l], gla_norm_g[l])
        osb = _stickbreak(hb3)
        x2d = _post(x2d, og.reshape(M, D), osb.reshape(M, D), hb, mod3, w_out[l].astype(BF16),
                    w_ffn_in[l].astype(BF16), w_ffn_out[l].astype(BF16),
                    ln1_g[l], ln1_b[l], ln2_g[l], ln2_b[l], S)
    return x2d.reshape(B, S, D)
```

```python
import functools
import math

import numpy as np
import jax
import jax.numpy as jnp
from jax import lax
from jax.experimental import pallas as pl
from jax.experimental.pallas import tpu as pltpu

F32 = jnp.float32
BF16 = jnp.bfloat16

D_MODEL = 1024
GLA_HEADS = 4
GLA_DK = 128
GLA_DV = 256
GLA_QK = GLA_HEADS * GLA_DK
GLA_VW = GLA_HEADS * GLA_DV
GLA_LOWRANK = 16
GLA_TAU = 16.0
GLA_CHUNK = 64
SB_HEADS = 8
SB_DH = 128
D_FF = 2816
DEPTH = 1
ALPHA = (2.0 * DEPTH) ** 0.25
LN_EPS = 1e-5
LOG2E = math.log2(math.e)

HB_W = 8 * D_MODEL
IN_GA_ROW = 2 * GLA_QK + 2 * GLA_VW
IN_TM = 512
IN_TN = 1024
HB_GQK, HB_GV, HB_GG, HB_SQ, HB_SK, HB_SV, HB_MA, HB_MB = 0, 1, 2, 3, 4, 5, 6, 7

VMEM_LIMIT = 56 * 1024 * 1024
LN_ROWS = 128


def _ln(x):
    mu = jnp.mean(x, axis=-1, keepdims=True)
    xc = x - mu
    var = jnp.mean(xc * xc, axis=-1, keepdims=True)
    return xc * lax.rsqrt(var + LN_EPS)


def _sigmoid(x):
    return 1.0 / (1.0 + jnp.exp(-x))


def _log2_sigmoid2(x2):
    return jnp.minimum(x2, 0.0) - jnp.log(1.0 + jnp.exp2(-jnp.abs(x2))) * LOG2E


def _adaln_kernel(c_ref, w_ref, b_ref, o_ref):
    c = c_ref[...]
    ca = c * _sigmoid(c)
    o_ref[...] = jnp.dot(ca, w_ref[...], preferred_element_type=F32,
                         precision=lax.Precision.HIGHEST) + b_ref[...]


def _adaln_mod(c, w_ada, b_ada):
    B, D = c.shape
    N = w_ada.shape[1]
    tn = 1024
    return pl.pallas_call(
        _adaln_kernel,
        grid=(N // tn,),
        in_specs=[pl.BlockSpec((B, D), lambda j: (0, 0)),
                  pl.BlockSpec((D, tn), lambda j: (0, j)),
                  pl.BlockSpec((1, tn), lambda j: (0, j))],
        out_specs=pl.BlockSpec((B, tn), lambda j: (0, j)),
        out_shape=jax.ShapeDtypeStruct((B, N), F32),
        name="adaln_mod",
    )(c, w_ada, b_ada.reshape(1, N))


def _decay_weight_kernel(wga_ref, wup_ref, o_ref):
    o_ref[...] = lax.dot_general(wga_ref[...], wup_ref[...], (((0,), (0,)), ((), ())),
                                 preferred_element_type=F32, precision=lax.Precision.HIGHEST)


def _decay_weight(w_ga, w_up):
    D = w_ga.shape[1]
    return pl.pallas_call(
        _decay_weight_kernel,
        out_shape=jax.ShapeDtypeStruct((D, GLA_QK), F32),
        name="decay_weight",
    )(w_ga, w_up)


def _inproj_kernel(x_ref, mod_ref, w_ref, cs_ref, wa_ref, hb_ref, xa_ref, u_s):
    t = pl.program_id(0)
    n_col = HB_W // IN_TN
    n_rows = x_ref.shape[0] // LN_ROWS

    @pl.when(t == 0)
    def _():
        u_s[...] = jnp.zeros_like(u_s)

    def ln_rows(slot, r):
        rs = slice(r * LN_ROWS, (r + 1) * LN_ROWS)
        shift = mod_ref[0, :, 0 * D_MODEL:1 * D_MODEL]
        scale1p = 1.0 + mod_ref[0, :, 1 * D_MODEL:2 * D_MODEL]
        u_s[slot, rs, :] = (_ln(x_ref[rs, :]) * scale1p + shift).astype(BF16)

    def step(ln_slot, mm_slot):
        u = u_s[mm_slot]
        for c in range(n_col):
            cols = slice(c * IN_TN, (c + 1) * IN_TN)
            r0 = c * IN_TN + (GLA_LOWRANK if c * IN_TN >= IN_GA_ROW else 0)
            acc = lax.dot_general(u, w_ref[r0:r0 + IN_TN, :], (((1,), (1,)), ((), ())),
                                  preferred_element_type=F32) * cs_ref[:, cols]
            hb_ref[:, cols] = acc.astype(BF16)
            if c < n_rows:
                ln_rows(ln_slot, c)
        xa_ref[...] = jnp.dot(u, wa_ref[...], preferred_element_type=F32) * LOG2E

    @pl.when(t % 2 == 0)
    def _():
        step(0, 1)

    @pl.when(t % 2 == 1)
    def _():
        step(1, 0)


def _in_proj(x2d, mod3, w_all, colscale, w_decay, seq):
    M, D = x2d.shape
    tm = IN_TM
    n = M // tm
    per_b = seq // tm
    assert tm // LN_ROWS <= HB_W // IN_TN
    ln_tile = lambda t: jnp.minimum(t, n - 1)
    mm_tile = lambda t: jnp.maximum(t - 1, 0)
    once = pl.Buffered(1)
    return pl.pallas_call(
        _inproj_kernel,
        grid=(n + 1,),
        in_specs=[pl.BlockSpec((tm, D), lambda t: (ln_tile(t), 0)),
                  pl.BlockSpec((1, 1, 6 * D_MODEL), lambda t: (ln_tile(t) // per_b, 0, 0)),
                  pl.BlockSpec((HB_W + GLA_LOWRANK, D), lambda t: (0, 0), pipeline_mode=once),
                  pl.BlockSpec((1, HB_W), lambda t: (0, 0), pipeline_mode=once),
                  pl.BlockSpec((D, GLA_QK), lambda t: (0, 0), pipeline_mode=once)],
        out_specs=[pl.BlockSpec((tm, HB_W), lambda t: (mm_tile(t), 0)),
                   pl.BlockSpec((tm, GLA_QK), lambda t: (mm_tile(t), 0))],
        out_shape=[jax.ShapeDtypeStruct((M, HB_W), BF16),
                   jax.ShapeDtypeStruct((M, GLA_QK), F32)],
        scratch_shapes=[pltpu.VMEM((2, tm, D), BF16)],
        compiler_params=pltpu.CompilerParams(
            dimension_semantics=("arbitrary",), vmem_limit_bytes=VMEM_LIMIT),
        name="in_proj",
    )(x2d, mod3, w_all, colscale, w_decay)


GLA_LEVELS = (32, 16, 8, 4, 2, 1)
GLA_FAST_MAX_DECAY = 60.0
GLA_FAST_MIN_X2 = 1.0 - GLA_FAST_MAX_DECAY * GLA_TAU / GLA_CHUNK


def _gla_cum_matrix():
    C = GLA_CHUNK
    r = np.arange(C)[:, None]
    s = np.arange(C)[None, :]
    blocks = [s <= r]
    for blk in GLA_LEVELS:
        blocks.append(s <= (r // (2 * blk)) * (2 * blk) + blk - 1)
    one = np.concatenate(blocks, axis=0).astype(np.float32)
    return np.concatenate([one, one, one], axis=1)


def _gla_pair_masks():
    C = GLA_CHUNK
    r = np.arange(C)[:, None]
    s = np.arange(C)[None, :]
    masks = [r == s]
    for blk in GLA_LEVELS:
        masks.append(((r // blk) == (s // blk) + 1) & ((r // (2 * blk)) == (s // (2 * blk))))
    masks.append(s <= r)
    return np.stack(masks).astype(np.float32)


def _gla_kernel(q_ref, k_ref, xa_ref, v_ref, gg_ref, ba_ref, gn_ref, pm_ref, cum_ref, o_ref,
                state_ref, b_s, ql_s, kl_s, p_s, qe_s, kd_s, dec_s, *, ts):
    C = GLA_CHUNK
    n_levels = len(GLA_LEVELS)
    n_chunks = ts // C
    nt = (((1,), (1,)), ((), ()))
    kslices = [slice(h * GLA_DK, (h + 1) * GLA_DK) for h in range(GLA_HEADS)]
    vslices = [slice(h * GLA_DV, (h + 1) * GLA_DV) for h in range(GLA_HEADS)]

    @pl.when(pl.program_id(1) == 0)
    def _():
        state_ref[...] = jnp.zeros_like(state_ref)

    ba2 = ba_ref[...] * LOG2E

    def rows(n):
        return pl.ds(n * C if isinstance(n, int) else pl.multiple_of(n * C, C), C)

    def decay_stage(n, slot, fast):
        la = _log2_sigmoid2(xa_ref[0, rows(n), :] + ba2) * (1.0 / GLA_TAU)
        la_hi = la.astype(BF16)
        rem = la - la_hi.astype(F32)
        la_mid = rem.astype(BF16)
        la_lo = (rem - la_mid.astype(F32)).astype(BF16)
        la3 = jnp.concatenate([la_hi, la_mid, la_lo], axis=0)
        if fast:
            ball = jnp.dot(cum_ref[0:C, :], la3, preferred_element_type=F32)
            for h, ks in enumerate(kslices):
                b_s[slot, h] = ball[:, ks]
            return
        ball = jnp.dot(cum_ref[...], la3, preferred_element_type=F32)
        q = q_ref[0, rows(n), :].astype(F32)
        k = k_ref[0, rows(n), :].astype(F32)
        for h, ks in enumerate(kslices):
            b = ball[0:C, ks]
            b_s[slot, h] = b
            for li in range(n_levels):
                f = jnp.exp2(-jnp.abs(b - ball[(li + 1) * C:(li + 2) * C, ks]))
                ql_s[slot, h, li] = (q[:, ks] * f).astype(BF16)
                kl_s[slot, h, li] = (k[:, ks] * f).astype(BF16)

    def score_stage(n, slot, fast):
        q16 = q_ref[0, rows(n), :]
        k16 = k_ref[0, rows(n), :]
        q = q16.astype(F32)
        k = k16.astype(F32)
        for h, ks in enumerate(kslices):
            b = b_s[slot, h]
            b_last = b[C - 1:C, :]
            qe = (q[:, ks] * jnp.exp2(b)).astype(BF16)
            if fast:
                ki = (k[:, ks] * jnp.exp2(-b)).astype(BF16)
                sc = lax.dot_general(qe, ki, nt, preferred_element_type=F32)
                p = jnp.where(pm_ref[n_levels + 1] != 0.0, sc, 0.0)
            else:
                sc = lax.dot_general(q16[:, ks], k16[:, ks], nt, preferred_element_type=F32)
                p = jnp.where(pm_ref[0] != 0.0, sc, 0.0)
                for li in range(n_levels):
                    sc = lax.dot_general(ql_s[slot, h, li], kl_s[slot, h, li], nt,
                                         preferred_element_type=F32)
                    p = jnp.where(pm_ref[li + 1] != 0.0, sc, p)
            p_s[slot, h] = p.astype(BF16)
            qe_s[slot, h] = qe
            kd_s[slot, h] = (k[:, ks] * jnp.exp2(b_last - b)).astype(BF16)
            dec_s[slot, h] = jnp.exp2(b_last)

    def output_stage(n, slot):
        v = v_ref[0, rows(n), :]
        gg = gg_ref[0, rows(n), :].astype(F32)
        for h, vs in enumerate(vslices):
            st = state_ref[h]
            vh = v[:, vs]
            o = (lax.dot_general(qe_s[slot, h], st.astype(BF16), nt, preferred_element_type=F32)
                 + jnp.dot(p_s[slot, h], vh, preferred_element_type=F32))
            upd = lax.dot_general(vh, kd_s[slot, h], (((0,), (0,)), ((), ())),
                                  preferred_element_type=F32)
            state_ref[h] = st * dec_s[slot, h] + upd
            on = o * lax.rsqrt(jnp.mean(o * o, axis=-1, keepdims=True) + LN_EPS)
            g = gg[:, vs]
            on = on * gn_ref[:, vs] * (g * _sigmoid(g))
            o_ref[0, rows(n), vs] = on.astype(BF16)

    def run(fast):
        decay_stage(0, 0, fast)
        decay_stage(1, 1, fast)
        score_stage(0, 0, fast)

        def pair(p, carry):
            n = 2 * p
            decay_stage(n + 2, 0, fast)
            score_stage(n + 1, 1, fast)
            output_stage(n, 0)
            decay_stage(n + 3, 1, fast)
            score_stage(n + 2, 0, fast)
            output_stage(n + 1, 1)
            return carry

        assert n_chunks % 2 == 0
        lax.fori_loop(0, (n_chunks - 2) // 2, pair, 0)
        score_stage(n_chunks - 1, 1, fast)
        output_stage(n_chunks - 2, 0)
        output_stage(n_chunks - 1, 1)

    safe = jnp.min(xa_ref[0] + ba2) >= GLA_FAST_MIN_X2

    @pl.when(safe)
    def _():
        run(True)

    @pl.when(jnp.logical_not(safe))
    def _():
        run(False)


def _gla(hb3, xa3, b_alpha, gla_norm_g):
    B, S, _ = hb3.shape
    ts = 2048
    kern = functools.partial(_gla_kernel, ts=ts)
    n_masks = 2 + len(GLA_LEVELS)
    H, C, L = GLA_HEADS, GLA_CHUNK, len(GLA_LEVELS)
    return pl.pallas_call(
        kern,
        grid=(B, S // ts),
        in_specs=[pl.BlockSpec((1, ts, GLA_QK), lambda b, s: (b, s, 2 * HB_GQK)),
                  pl.BlockSpec((1, ts, GLA_QK), lambda b, s: (b, s, 2 * HB_GQK + 1)),
                  pl.BlockSpec((1, ts, GLA_QK), lambda b, s: (b, s, 0)),
                  pl.BlockSpec((1, ts, GLA_VW), lambda b, s: (b, s, HB_GV)),
                  pl.BlockSpec((1, ts, GLA_VW), lambda b, s: (b, s, HB_GG)),
                  pl.BlockSpec((1, GLA_QK), lambda b, s: (0, 0)),
                  pl.BlockSpec((1, GLA_VW), lambda b, s: (0, 0)),
                  pl.BlockSpec((n_masks, C, C), lambda b, s: (0, 0, 0)),
                  pl.BlockSpec(((1 + L) * C, 3 * C), lambda b, s: (0, 0))],
        out_specs=pl.BlockSpec((1, ts, GLA_VW), lambda b, s: (b, s, 0)),
        out_shape=jax.ShapeDtypeStruct((B, S, GLA_VW), BF16),
        scratch_shapes=[pltpu.VMEM((H, GLA_DV, GLA_DK), F32),
                        pltpu.VMEM((2, H, C, GLA_DK), F32),
                        pltpu.VMEM((2, H, L, C, GLA_DK), BF16),
                        pltpu.VMEM((2, H, L, C, GLA_DK), BF16),
                        pltpu.VMEM((2, H, C, C), BF16),
                        pltpu.VMEM((2, H, C, GLA_DK), BF16),
                        pltpu.VMEM((2, H, C, GLA_DK), BF16),
                        pltpu.VMEM((2, H, 1, GLA_DK), F32)],
        compiler_params=pltpu.CompilerParams(
            dimension_semantics=("arbitrary", "arbitrary"), vmem_limit_bytes=VMEM_LIMIT),
        name="gla",
    )(hb3, hb3, xa3, hb3, hb3, b_alpha.reshape(1, GLA_QK), gla_norm_g.reshape(1, GLA_VW),
      jnp.asarray(_gla_pair_masks()), jnp.asarray(_gla_cum_matrix(), dtype=BF16))


SB_TQ = 128
SB_SUB = 128
SB_NSUB = 3
SB_HP = 4
SB_MASKED = -1e30
SB_DEAD2 = -152.0


def _sb_suffix_matrix():
    j = np.arange(2 * SB_SUB)[:, None] % SB_SUB
    s = np.arange(2 * SB_SUB)[None, :]
    return ((s >= SB_SUB) | (j > s)).astype(np.float32)


def _sb_window_bias():
    r = np.arange(SB_TQ)[:, None]
    c = np.arange(SB_NSUB * SB_SUB)[None, :]
    return np.stack([np.where(c < r + d * SB_SUB, 0.0, SB_MASKED)
                     for d in range(SB_NSUB)]).astype(np.float32)


def _sb_scores(q, k):
    return lax.dot_general(q, k, (((1,), (1,)), ((), ())), preferred_element_type=F32)


def _sb_split(z):
    n_sub = z.shape[1] // SB_SUB
    ls = _log2_sigmoid2(z)
    lk = ls - z
    hi = lk.astype(BF16)
    lo = (lk - hi.astype(F32)).astype(BF16)
    parts = []
    for c in range(n_sub):
        cs = slice(c * SB_SUB, (c + 1) * SB_SUB)
        parts += [hi[:, cs], lo[:, cs]]
    return ls, jnp.concatenate(parts, axis=1)


def _sb_weights(ls, hl, carry, mo):
    n_sub = ls.shape[1] // SB_SUB
    ws = [None] * n_sub
    for c in reversed(range(n_sub)):
        r = jnp.dot(hl[:, 2 * c * SB_SUB:2 * (c + 1) * SB_SUB], mo, preferred_element_type=F32)
        after = r[:, :SB_SUB] if carry is None else r[:, :SB_SUB] + carry
        ws[c] = jnp.exp2(ls[:, c * SB_SUB:(c + 1) * SB_SUB] + after).astype(BF16)
        carry = r[:, SB_SUB:] if carry is None else carry + r[:, SB_SUB:]
    return (jnp.concatenate(ws, axis=1) if n_sub > 1 else ws[0]), carry


def _sb_kernel(q_ref, k_ref, v_ref, mo_ref, bias_ref, o_ref, z_s, ls_s, hl_s, w_s, c_s, call_s,
               live_s):
    T = SB_TQ
    W = SB_NSUB * SB_SUB
    n_q = q_ref.shape[1] // T
    heads = [slice(h * SB_DH, (h + 1) * SB_DH) for h in range(SB_HP)]

    def row0(i):
        return i * T if isinstance(i, int) else pl.multiple_of(i * T, T)

    def window(i):
        if isinstance(i, int):
            return max(i + 1 - SB_NSUB, 0) * T, min(i, SB_NSUB - 1)
        return row0(jnp.maximum(i + 1 - SB_NSUB, 0)), jnp.minimum(i, SB_NSUB - 1)

    def scores_stage(i, slot):
        q0 = row0(i)
        k0, d = window(i)
        for h, hs in enumerate(heads):
            z = _sb_scores(q_ref[0, pl.ds(q0, T), hs], k_ref[0, pl.ds(k0, W), hs])
            if isinstance(i, int) and i < SB_NSUB - 1:
                z_s[slot, h] = z + bias_ref[d]
            else:
                z_s[slot, h] = jnp.concatenate(
                    [z[:, :W - SB_SUB],
                     z[:, W - SB_SUB:] + bias_ref[SB_NSUB - 1, :, W - SB_SUB:]], axis=1)

    def split_stage(slot):
        for h in range(SB_HP):
            ls, hl = _sb_split(z_s[slot, h])
            ls_s[slot, h] = ls
            hl_s[slot, h] = hl

    def weights_stage(i, slot):
        for h in range(SB_HP):
            w, c = _sb_weights(ls_s[slot, h], hl_s[slot, h], None, mo_ref[...])
            w_s[slot, h] = w
            c_s[slot, h] = c
            call_s[i, h] = c

    def output_stage(i, slot):
        q0 = row0(i)
        k0, _ = window(i)
        for h, hs in enumerate(heads):
            acc = jnp.dot(w_s[slot, h], v_ref[0, pl.ds(k0, W), hs], preferred_element_type=F32)
            o_ref[0, pl.ds(q0, T), hs] = acc.astype(BF16)
        live = functools.reduce(jnp.maximum, [c_s[slot, h] for h in range(SB_HP)])
        live_s[i] = jnp.max(live)

    def steady(j, s):
        scores_stage(j, s)
        split_stage(1 - s)
        weights_stage(j - 2, s)
        output_stage(j - 3, 1 - s)

    scores_stage(0, 0)
    scores_stage(1, 1)
    split_stage(0)
    scores_stage(2, 0)
    split_stage(1)
    weights_stage(0, 0)

    def pair(p, carry):
        j = 3 + 2 * p
        steady(j, 1)
        steady(j + 1, 0)
        return carry

    n_pairs = (n_q - 3) // 2
    lax.fori_loop(0, n_pairs, pair, 0)
    for j in range(3 + 2 * n_pairs, n_q):
        steady(j, j % 2)
    last = (n_q - 1) % 2
    split_stage(last)
    weights_stage(n_q - 2, 1 - last)
    output_stage(n_q - 3, last)
    weights_stage(n_q - 1, last)
    output_stage(n_q - 2, 1 - last)
    output_stage(n_q - 1, last)

    def continue_tile(i):
        q0 = row0(i)
        k0, _ = window(i)
        qs = [q_ref[0, pl.ds(q0, T), hs] for hs in heads]

        def cond(st):
            kk, carries, _ = st
            live = functools.reduce(jnp.maximum, carries)
            return jnp.logical_and(kk >= 0, jnp.max(live) > SB_DEAD2)

        def body(st):
            kk, carries, accs = st
            kk = pl.multiple_of(kk, SB_SUB)
            new_c, new_a = [], []
            for h, hs in enumerate(heads):
                ls, hl = _sb_split(_sb_scores(qs[h], k_ref[0, pl.ds(kk, SB_SUB), hs]))
                w, c = _sb_weights(ls, hl, carries[h], mo_ref[...])
                new_c.append(c)
                new_a.append(accs[h] + jnp.dot(w, v_ref[0, pl.ds(kk, SB_SUB), hs],
                                               preferred_element_type=F32))
            return kk - SB_SUB, tuple(new_c), tuple(new_a)

        carries = tuple(call_s[i, h] for h in range(SB_HP))
        accs = tuple(o_ref[0, pl.ds(q0, T), hs].astype(F32) for hs in heads)
        _, _, accs = lax.while_loop(cond, body, (k0 - SB_SUB, carries, accs))
        for h, hs in enumerate(heads):
            o_ref[0, pl.ds(q0, T), hs] = accs[h].astype(BF16)

    def second_pass(i, carry):
        @pl.when(live_s[i] > SB_DEAD2)
        def _():
            continue_tile(i)
        return carry

    lax.fori_loop(SB_NSUB, n_q, second_pass, 0)


def _stickbreak(hb3):
    B, S, _ = hb3.shape
    wblk = SB_HP * SB_DH
    per_group = D_MODEL // wblk
    blk = lambda group: pl.BlockSpec((1, S, wblk), lambda b, g: (b, 0, group * per_group + g))
    const = lambda a: pl.BlockSpec(a.shape, lambda b, g: (0, 0))
    mo = jnp.asarray(_sb_suffix_matrix(), dtype=BF16)
    bias = jnp.asarray(_sb_window_bias())
    win = SB_NSUB * SB_SUB
    return pl.pallas_call(
        _sb_kernel,
        grid=(B, SB_HEADS // SB_HP),
        in_specs=[blk(HB_SQ), blk(HB_SK), blk(HB_SV), const(mo),
                  pl.BlockSpec(bias.shape, lambda b, g: (0, 0, 0))],
        out_specs=pl.BlockSpec((1, S, wblk), lambda b, g: (b, 0, g)),
        out_shape=jax.ShapeDtypeStruct((B, S, SB_HEADS * SB_DH), BF16),
        scratch_shapes=[pltpu.VMEM((2, SB_HP, SB_TQ, win), F32),
                        pltpu.VMEM((2, SB_HP, SB_TQ, win), F32),
                        pltpu.VMEM((2, SB_HP, SB_TQ, 2 * win), BF16),
                        pltpu.VMEM((2, SB_HP, SB_TQ, win), BF16),
                        pltpu.VMEM((2, SB_HP, SB_TQ, SB_SUB), F32),
                        pltpu.VMEM((S // SB_TQ, SB_HP, SB_TQ, SB_SUB), F32),
                        pltpu.SMEM((S // SB_TQ,), F32)],
        compiler_params=pltpu.CompilerParams(
            dimension_semantics=("arbitrary", "arbitrary"), vmem_limit_bytes=VMEM_LIMIT),
        name="stickbreak",
    )(hb3, hb3, hb3, mo, bias)


POST_TM = 512
FFN_BLK = 256
POST_ROWS = 128


def _post_kernel(x_ref, og_ref, os_ref, ma_ref, mb_ref, mod_e_ref, mod_l_ref, wout_ref, wg_ref,
                 wu_ref, wo_ref, g1_ref, b1_ref, g2_ref, b2_ref, o_ref, x1_s, u2_s, a_s):
    t = pl.program_id(0)

    @pl.when(t == 0)
    def _():
        x1_s[...] = jnp.zeros_like(x1_s)
        u2_s[...] = jnp.zeros_like(u2_s)

    n_ff = D_FF // FFN_BLK
    n_rows = x_ref.shape[0] // POST_ROWS

    def early_rows(slot, r):
        rs = slice(r * POST_ROWS, (r + 1) * POST_ROWS)
        merged = (_sigmoid(ma_ref[rs, :].astype(F32)) * og_ref[rs, :].astype(F32)
                  + _sigmoid(mb_ref[rs, :].astype(F32)) * os_ref[rs, :].astype(F32))
        y = jnp.dot(merged.astype(BF16), wout_ref[...], preferred_element_type=F32)
        gate1 = mod_e_ref[0, :, 2 * D_MODEL:3 * D_MODEL]
        shift2 = mod_e_ref[0, :, 3 * D_MODEL:4 * D_MODEL]
        scale2 = mod_e_ref[0, :, 4 * D_MODEL:5 * D_MODEL]
        x1 = _ln(ALPHA * x_ref[rs, :] + (1.0 + gate1) * y) * g1_ref[...] + b1_ref[...]
        x1_s[slot, rs, :] = x1
        u2_s[slot, rs, :] = (_ln(x1) * (1.0 + scale2) + shift2).astype(BF16)

    def ffn_block(slot, f):
        cols = slice(f * FFN_BLK, (f + 1) * FFN_BLK)
        u = u2_s[slot]
        g = jnp.dot(u, wg_ref[:, cols], preferred_element_type=F32)
        up = jnp.dot(u, wu_ref[:, cols], preferred_element_type=F32)
        a_s[:, cols] = (g * _sigmoid(g) * up).astype(BF16)

    def out_rows(slot, r):
        rs = slice(r * POST_ROWS, (r + 1) * POST_ROWS)
        ff = jnp.dot(a_s[rs, :], wo_ref[...], preferred_element_type=F32)
        gate2 = mod_l_ref[0, :, 5 * D_MODEL:6 * D_MODEL]
        o_ref[rs, :] = (_ln(ALPHA * x1_s[slot, rs, :] + (1.0 + gate2) * ff) * g2_ref[...]
                        + b2_ref[...])

    def step(e_slot, l_slot):
        every = (n_ff + n_rows - 1) // n_rows
        for f in range(n_ff):
            ffn_block(l_slot, f)
            if f % every == 0 and f // every < n_rows:
                early_rows(e_slot, f // every)
        for r in range(n_rows):
            out_rows(l_slot, r)

    @pl.when(t % 2 == 0)
    def _():
        step(0, 1)

    @pl.when(t % 2 == 1)
    def _():
        step(1, 0)


def _post(x2d, og2, os2, hb2, mod3, w_out, w_ffn_in, w_ffn_out, ln1_g, ln1_b, ln2_g, ln2_b, seq):
    M, D = x2d.shape
    tm = POST_TM
    n = M // tm
    per_b = seq // tm
    e_tile = lambda t: jnp.minimum(t, n - 1)
    l_tile = lambda t: jnp.maximum(t - 1, 0)
    once = pl.Buffered(1)
    row_e = lambda c: pl.BlockSpec((tm, D), lambda t: (e_tile(t), c))
    vec = pl.BlockSpec((1, D), lambda t: (0, 0), pipeline_mode=once)
    return pl.pallas_call(
        _post_kernel,
        grid=(n + 1,),
        in_specs=[row_e(0), row_e(0), row_e(0), row_e(HB_MA), row_e(HB_MB),
                  pl.BlockSpec((1, 1, 6 * D_MODEL), lambda t: (e_tile(t) // per_b, 0, 0)),
                  pl.BlockSpec((1, 1, 6 * D_MODEL), lambda t: (l_tile(t) // per_b, 0, 0)),
                  pl.BlockSpec((D, D), lambda t: (0, 0), pipeline_mode=once),
                  pl.BlockSpec((D, D_FF), lambda t: (0, 0), pipeline_mode=once),
                  pl.BlockSpec((D, D_FF), lambda t: (0, 1), pipeline_mode=once),
                  pl.BlockSpec((D_FF, D), lambda t: (0, 0), pipeline_mode=once),
                  vec, vec, vec, vec],
        out_specs=pl.BlockSpec((tm, D), lambda t: (l_tile(t), 0)),
        out_shape=jax.ShapeDtypeStruct((M, D), F32),
        scratch_shapes=[pltpu.VMEM((2, tm, D), F32),
                        pltpu.VMEM((2, tm, D), BF16),
                        pltpu.VMEM((tm, D_FF), BF16)],
        compiler_params=pltpu.CompilerParams(
            dimension_semantics=("arbitrary",), vmem_limit_bytes=VMEM_LIMIT),
        name="post",
    )(x2d, og2, os2, hb2, hb2, mod3, mod3, w_out, w_ffn_in, w_ffn_in, w_ffn_out,
      ln1_g.reshape(1, D), ln1_b.reshape(1, D), ln2_g.reshape(1, D), ln2_b.reshape(1, D))


def _regroup_w_in(w_in_l):
    w_t = w_in_l.T
    return w_t.astype(BF16), w_t[IN_GA_ROW:IN_GA_ROW + GLA_LOWRANK]


def _colscale():
    ones = lambda n: jnp.ones((n,), F32)
    return jnp.concatenate([
        ones(GLA_QK) * (GLA_DK ** -0.5), ones(GLA_QK), ones(2 * GLA_VW),
        ones(D_MODEL) * (SB_DH ** -0.5 * LOG2E), ones(4 * D_MODEL)]).reshape(1, HB_W)


def kernel(x, c, w_ada, b_ada, w_in, w_alpha_up, b_alpha, gla_norm_g, w_out,
           ln1_g, ln1_b, w_ffn_in, w_ffn_out, ln2_g, ln2_b):
    B, S, D = x.shape
    M = B * S
    x2d = x.reshape(M, D)
    for l in range(DEPTH):
        mod3 = _adaln_mod(c, w_ada[l], b_ada[l]).reshape(B, 1, 6 * D)
        w_all, w_ga = _regroup_w_in(w_in[l])
        w_decay = _decay_weight(w_ga, w_alpha_up[l]).astype(BF16)
        hb, xa = _in_proj(x2d, mod3, w_all, _colscale(), w_decay, S)
        hb3 = hb.reshape(B, S, HB_W)
        og = _gla(hb3, xa.reshape(B, S, GLA_QK), b_alpha[l], gla_norm_g[l])
        osb = _stickbreak(hb3)
        x2d = _post(x2d, og.reshape(M, D), osb.reshape(M, D), hb, mod3, w_out[l].astype(BF16),
                    w_ffn_in[l].astype(BF16), w_ffn_out[l].astype(BF16),
                    ln1_g[l], ln1_b[l], ln2_g[l], ln2_b[l], S)
    return x2d.reshape(B, S, D)
```
